```python
import jax, jax.numpy as jnp
from jax import lax
import numpy as np

D_MODEL = 2048
BATCH = 4
SEQ = 4096
DEPTH = 4

HEAD_DIM = 128
N_HEADS = D_MODEL // HEAD_DIM
N_HEADS_A = N_HEADS // 2
N_HEADS_B = N_HEADS - N_HEADS_A
N_HEADS_C = N_HEADS
DILATED_CONFIGS = ((128, 1), (512, 4), (2048, 16))
MOBA_BLOCK = 256
MOBA_TOPK = 3
MOBA_Q_CHUNK = 16
FOX_Q_BLOCK = 128
FORGET_BIAS_MEAN = 2.0
ROT_DIM = HEAD_DIM // 4
ROPE_THETA = 500000.0
MEM_TOKENS = 256
MEM_HEADS = 4
MEM_HEAD_DIM = 128
D_FF = ((8 * D_MODEL // 3 + 255) // 256) * 256
CONV_WIDTH = 3
RMS_EPS = 1e-6
N_EVEN = (DEPTH + 1) // 2
N_ODD = DEPTH // 2

kernel_name = "hybrid_dilated_moba_fox_trunk"


def rmsnorm(x, g):
    xf = x.astype(jnp.float32)
    y = xf * lax.rsqrt(jnp.mean(xf * xf, axis=-1, keepdims=True) + RMS_EPS)
    return (y * g.astype(jnp.float32)).astype(x.dtype)


def rope_tables(positions):
    inv_freq = ROPE_THETA ** (-jnp.arange(0, ROT_DIM, 2, dtype=jnp.float32) / ROT_DIM)
    ang = positions.astype(jnp.float32)[:, None, :, None] * inv_freq
    return jnp.cos(ang), jnp.sin(ang)


def apply_partial_rope(t, cos, sin):
    half = ROT_DIM // 2
    tf = t.astype(jnp.float32)
    r1, r2 = tf[..., :half], tf[..., half:ROT_DIM]
    rot = jnp.concatenate([r1 * cos - r2 * sin, r2 * cos + r1 * sin], axis=-1)
    return jnp.concatenate([rot.astype(t.dtype), t[..., ROT_DIM:]], axis=-1)


def dilated_branch(q, k, v, dil, n_back):
    B_, H, S, E = q.shape
    L = S // dil
    Q = n_back
    nb = -(-L // Q)
    Lp = nb * Q
    scale = HEAD_DIM ** -0.5

    def split(t):
        return t.reshape(B_, H, L, dil, E).transpose(0, 1, 3, 2, 4)

    qs, ks, vs = split(q), split(k), split(v)
    pad_q = ((0, 0), (0, 0), (0, 0), (0, Lp - L), (0, 0))
    pad_k = ((0, 0), (0, 0), (0, 0), (Q, Lp - L), (0, 0))
    qb = jnp.pad(qs, pad_q).reshape(B_, H, dil, nb, Q, E)
    kp = jnp.pad(ks, pad_k).reshape(B_, H, dil, nb + 1, Q, E)
    vp = jnp.pad(vs, pad_k).reshape(B_, H, dil, nb + 1, Q, E)
    kb = jnp.concatenate([kp[:, :, :, :-1], kp[:, :, :, 1:]], axis=4)
    vb = jnp.concatenate([vp[:, :, :, :-1], vp[:, :, :, 1:]], axis=4)

    s = jnp.einsum('bhrnqe,bhrnke->bhrnqk', qb, kb).astype(jnp.float32) * scale
    qi = jnp.arange(nb)[:, None, None] * Q + jnp.arange(Q)[None, :, None]
    kj = jnp.arange(nb)[:, None, None] * Q - Q + jnp.arange(2 * Q)[None, None, :]
    dist = qi - kj
    mask = (dist >= 0) & (dist <= Q) & (kj >= 0)
    s = jnp.where(mask, s, -jnp.inf)
    m = jnp.max(s, axis=-1, keepdims=True)
    e = jnp.exp(s - m)
    den = jnp.sum(e, axis=-1)
    o = jnp.einsum('bhrnqk,bhrnke->bhrnqe', e, vb.astype(jnp.float32)) / den[..., None]
    lse = m[..., 0] + jnp.log(den)

    o = o.reshape(B_, H, dil, Lp, E)[:, :, :, :L].transpose(0, 1, 3, 2, 4).reshape(B_, H, S, E)
    lse = lse.reshape(B_, H, dil, Lp)[:, :, :, :L].transpose(0, 1, 3, 2).reshape(B_, H, S)
    return o, lse


def dilated_mixture_attention(q, k, v):
    outs, lses = [], []
    for window, dil in DILATED_CONFIGS:
        o, l = dilated_branch(q, k, v, dil, window // dil)
        outs.append(o)
        lses.append(l)
    w = jax.nn.softmax(jnp.stack(lses), axis=0)
    o = jnp.sum(w[..., None] * jnp.stack(outs), axis=0)
    return o.astype(q.dtype)


def moba_attention(q, k, v):
    B_, H, S, E = q.shape
    BS = MOBA_BLOCK
    Tc = MOBA_Q_CHUNK
    scale = HEAD_DIM ** -0.5
    Sp = -(-S // BS) * BS
    nblk = Sp // BS
    pad = ((0, 0), (0, 0), (0, Sp - S), (0, 0))
    kb = jnp.pad(k, pad).reshape(B_, H, nblk, BS, E)
    vb = jnp.pad(v, pad).reshape(B_, H, nblk, BS, E)
    kbar = jnp.mean(kb.astype(jnp.float32), axis=3)
    ksel = min(MOBA_TOPK, nblk)
    nc = S // Tc
    qc = q.reshape(B_, H, nc, Tc, E).transpose(2, 0, 1, 3, 4)
    bi = jnp.arange(B_)[:, None, None, None]
    hi = jnp.arange(H)[None, :, None, None]
    blk_ids = jnp.arange(nblk)

    def chunk(args):
        qx, c = args
        t = c * Tc + jnp.arange(Tc)
        own = (c * Tc) // BS
        gate = jnp.einsum('bhqe,bhne->bhqn', qx.astype(jnp.float32), kbar)
        gate = jnp.where(blk_ids < own, gate, -jnp.inf)
        _, idx = lax.top_k(gate, ksel)
        valid = idx < own
        kg = kb[bi, hi, idx]
        vg = vb[bi, hi, idx]
        s_sel = jnp.einsum('bhqe,bhqnke->bhqnk', qx, kg).astype(jnp.float32) * scale
        s_sel = jnp.where(valid[..., None], s_sel, -jnp.inf).reshape(B_, H, Tc, ksel * BS)
        ko = lax.dynamic_index_in_dim(kb, own, axis=2, keepdims=False)
        vo = lax.dynamic_index_in_dim(vb, own, axis=2, keepdims=False)
        s_own = jnp.einsum('bhqe,bhke->bhqk', qx, ko).astype(jnp.float32) * scale
        kpos = own * BS + jnp.arange(BS)
        s_own = jnp.where(kpos[None, :] <= t[:, None], s_own, -jnp.inf)
        p = jax.nn.softmax(jnp.concatenate([s_sel, s_own], axis=-1), axis=-1)
        p_sel = p[..., :ksel * BS].reshape(B_, H, Tc, ksel, BS)
        p_own = p[..., ksel * BS:]
        o = (jnp.einsum('bhqnk,bhqnke->bhqe', p_sel, vg.astype(jnp.float32))
             + jnp.einsum('bhqk,bhke->bhqe', p_own, vo.astype(jnp.float32)))
        return o.astype(q.dtype)

    out = lax.map(chunk, (qc, jnp.arange(nc)))
    return out.transpose(1, 2, 0, 3, 4).reshape(B_, H, S, E)


def forgetting_attention(q, k, v, logf):
    B_, H, S, E = q.shape
    QB = FOX_Q_BLOCK
    scale = HEAD_DIM ** -0.5
    c = jnp.cumsum(logf, axis=-1)
    nq = S // QB
    qb = q.reshape(B_, H, nq, QB, E).transpose(2, 0, 1, 3, 4)
    cb = c.reshape(B_, H, nq, QB).transpose(2, 0, 1, 3)
    kpos = jnp.arange(S)

    def blk(args):
        qx, cx, i = args
        tpos = i * QB + jnp.arange(QB)
        s = (jnp.einsum('bhqe,bhke->bhqk', qx, k).astype(jnp.float32) * scale
             + cx[..., None] - c[:, :, None, :])
        s = jnp.where(kpos[None, :] <= tpos[:, None], s, -jnp.inf)
        p = jax.nn.softmax(s, axis=-1)
        return jnp.einsum('bhqk,bhke->bhqe', p, v.astype(jnp.float32)).astype(q.dtype)

    out = lax.map(blk, (qb, cb, jnp.arange(nq)))
    return out.transpose(1, 2, 0, 3, 4).reshape(B_, H, S, E)


def merge_heads(o):
    B_, H, S, E = o.shape
    return o.transpose(0, 2, 1, 3).reshape(B_, S, H * E)


def even_mixer(h, w_in, w_out, cos, sin):
    B_, S, _ = h.shape
    proj = h @ w_in
    parts = proj.reshape(B_, S, 6, N_HEADS_A, HEAD_DIM).transpose(2, 0, 3, 1, 4)
    qa, ka, va, qb, kb, vb = parts[0], parts[1], parts[2], parts[3], parts[4], parts[5]
    qa, ka = apply_partial_rope(qa, cos, sin), apply_partial_rope(ka, cos, sin)
    qb, kb = apply_partial_rope(qb, cos, sin), apply_partial_rope(kb, cos, sin)
    oa = dilated_mixture_attention(qa, ka, va)
    ob = moba_attention(qb, kb, vb)
    o = jnp.concatenate([oa, ob], axis=1)
    return merge_heads(o) @ w_out


def odd_mixer(h, w_in, b_f, w_out):
    B_, S, _ = h.shape
    proj = h @ w_in
    nqkv = 3 * N_HEADS_C * HEAD_DIM
    qkv = proj[..., :nqkv].reshape(B_, S, 3, N_HEADS_C, HEAD_DIM).transpose(2, 0, 3, 1, 4)
    fg = proj[..., nqkv:].astype(jnp.float32) + b_f.astype(jnp.float32)
    logf = jax.nn.log_sigmoid(fg).transpose(0, 2, 1)
    o = forgetting_attention(qkv[0], qkv[1], qkv[2], logf)
    return merge_heads(o) @ w_out


def memory_cross_attention(h, memn, wq, wk, wv, wo):
    B_, S, _ = h.shape
    M = memn.shape[1]
    scale = MEM_HEAD_DIM ** -0.5
    q = (h @ wq).reshape(B_, S, MEM_HEADS, MEM_HEAD_DIM)
    k = (memn @ wk).reshape(B_, M, MEM_HEADS, MEM_HEAD_DIM)
    v = (memn @ wv).reshape(B_, M, MEM_HEADS, MEM_HEAD_DIM)
    s = jnp.einsum('bshe,bmhe->bhsm', q, k).astype(jnp.float32) * scale
    p = jax.nn.softmax(s, axis=-1)
    o = jnp.einsum('bhsm,bmhe->bshe', p, v.astype(jnp.float32)).astype(h.dtype)
    return o.reshape(B_, S, MEM_HEADS * MEM_HEAD_DIM) @ wo


def conv_ffn(h, w_up, conv_w, conv_b, w_down):
    S = h.shape[1]
    u = h @ w_up
    up = jnp.pad(u, ((0, 0), (CONV_WIDTH - 1, 0), (0, 0)))
    u = conv_b + conv_w[0] * up[:, 0:S] + conv_w[1] * up[:, 1:S + 1] + conv_w[2] * up[:, 2:S + 2]
    g, val = u[..., :D_FF], u[..., D_FF:]
    return (jax.nn.silu(g) * val).astype(h.dtype) @ w_down


def setup_inputs(seed: int = 0) -> dict:
    key = jax.random.key(seed)
    ks = jax.random.split(key, 24)
    f32 = jnp.float32

    def nrm(k, shape, fan_in):
        return jax.random.normal(k, shape, f32) * (fan_in ** -0.5)

    def gain(k, shape):
        return 1.0 + 0.02 * jax.random.normal(k, shape, f32)

    w_ab = (N_HEADS_A + N_HEADS_B) * HEAD_DIM
    w_c = N_HEADS_C * HEAD_DIM
    w_m = MEM_HEADS * MEM_HEAD_DIM
    x = jax.random.normal(ks[0], (BATCH, SEQ, D_MODEL), f32)
    mem = jax.random.normal(ks[1], (BATCH, MEM_TOKENS, D_MODEL), f32)
    positions = (jax.random.randint(ks[2], (BATCH, 1), 0, 1024, dtype=jnp.int32)
                 + jnp.arange(SEQ, dtype=jnp.int32)[None, :])
    return {
        "x": x,
        "mem": mem,
        "positions": positions,
        "w_in_ab": nrm(ks[3], (N_EVEN, D_MODEL, 3 * w_ab), D_MODEL),
        "w_out_ab": nrm(ks[4], (N_EVEN, w_ab, D_MODEL), w_ab),
        "w_in_c": nrm(ks[5], (N_ODD, D_MODEL, 3 * w_c + N_HEADS_C), D_MODEL),
        "b_f": FORGET_BIAS_MEAN + 0.5 * jax.random.normal(ks[6], (N_ODD, N_HEADS_C), f32),
        "w_out_c": nrm(ks[7], (N_ODD, w_c, D_MODEL), w_c),
        "g_mix_pre": gain(ks[8], (DEPTH, D_MODEL)),
        "g_mix_post": gain(ks[9], (DEPTH, D_MODEL)),
        "g_mem_kv": gain(ks[10], (DEPTH, D_MODEL)),
        "g_mem_pre": gain(ks[11], (DEPTH, D_MODEL)),
        "g_mem_post": gain(ks[12], (DEPTH, D_MODEL)),
        "w_mq": nrm(ks[13], (DEPTH, D_MODEL, w_m), D_MODEL),
        "w_mk": nrm(ks[14], (DEPTH, D_MODEL, w_m), D_MODEL),
        "w_mv": nrm(ks[15], (DEPTH, D_MODEL, w_m), D_MODEL),
        "w_mo": nrm(ks[16], (DEPTH, w_m, D_MODEL), w_m),
        "g_ffn_pre": gain(ks[17], (DEPTH, D_MODEL)),
        "g_ffn_post": gain(ks[18], (DEPTH, D_MODEL)),
        "w_up": nrm(ks[19], (DEPTH, D_MODEL, 2 * D_FF), D_MODEL),
        "conv_w": nrm(ks[20], (DEPTH, CONV_WIDTH, 2 * D_FF), CONV_WIDTH),
        "conv_b": 0.02 * jax.random.normal(ks[21], (DEPTH, 2 * D_FF), f32),
        "w_down": nrm(ks[22], (DEPTH, D_FF, D_MODEL), D_FF),
    }


def reference(x, mem, positions, w_in_ab, w_out_ab, w_in_c, b_f, w_out_c,
              g_mix_pre, g_mix_post, g_mem_kv, g_mem_pre, g_mem_post,
              w_mq, w_mk, w_mv, w_mo, g_ffn_pre, g_ffn_post,
              w_up, conv_w, conv_b, w_down):
    cos, sin = rope_tables(positions)
    for layer in range(DEPTH):
        h = rmsnorm(x, g_mix_pre[layer])
        if layer % 2 == 0:
            y = even_mixer(h, w_in_ab[layer // 2], w_out_ab[layer // 2], cos, sin)
        else:
            y = odd_mixer(h, w_in_c[layer // 2], b_f[layer // 2], w_out_c[layer // 2])
        x = x + rmsnorm(y, g_mix_post[layer])
        h = rmsnorm(x, g_mem_pre[layer])
        memn = rmsnorm(mem, g_mem_kv[layer])
        y = memory_cross_attention(h, memn, w_mq[layer], w_mk[layer], w_mv[layer], w_mo[layer])
        x = x + rmsnorm(y, g_mem_post[layer])
        h = rmsnorm(x, g_ffn_pre[layer])
        y = conv_ffn(h, w_up[layer], conv_w[layer], conv_b[layer], w_down[layer])
        x = x + rmsnorm(y, g_ffn_post[layer])
    return x
```

```python
import functools

import numpy as np
import jax
import jax.numpy as jnp
from jax import lax
from jax.experimental import pallas as pl
from jax.experimental.pallas import tpu as pltpu

D_MODEL = 2048
HEAD_DIM = 128
N_HEADS = D_MODEL // HEAD_DIM
N_HEADS_A = N_HEADS // 2
N_HEADS_B = N_HEADS - N_HEADS_A
DILATED_CONFIGS = ((128, 1), (512, 4), (2048, 16))
MOBA_BLOCK = 256
MOBA_TOPK = 3
ROT_DIM = HEAD_DIM // 4
ROPE_THETA = 500000.0
MEM_HEADS = 4
MEM_HEAD_DIM = 128
D_FF = ((8 * D_MODEL // 3 + 255) // 256) * 256
CONV_WIDTH = 3
RMS_EPS = 1e-6

LANES = 128
SUBLANES = 8
NEG = -1e30
F32 = jnp.float32
BF16 = jnp.bfloat16
MIB = 1024 * 1024


def _params(semantics, vmem_mib):
    return pltpu.CompilerParams(dimension_semantics=semantics,
                                vmem_limit_bytes=vmem_mib * MIB)


def _rms(x, g):
    ms = jnp.mean(x * x, axis=-1, keepdims=True)
    return x * lax.rsqrt(ms + RMS_EPS) * g


def _dot(a, b):
    return jnp.dot(a, b, preferred_element_type=F32)


def _dot_nt(a, b):
    return lax.dot_general(a, b, (((1,), (1,)), ((), ())), preferred_element_type=F32)


def _softmax_step(s, v, m_ref, l_ref, acc_ref):
    m_prev = m_ref[...]
    m_new = jnp.maximum(m_prev, jnp.max(s, axis=1, keepdims=True))
    p = jnp.exp(s - m_new)
    alpha = jnp.exp(m_prev - m_new)
    l_ref[...] = alpha * l_ref[...] + jnp.sum(p, axis=1, keepdims=True)
    acc_ref[...] = alpha * acc_ref[...] + _dot(p.astype(BF16), v)
    m_ref[...] = m_new


def _softmax_init(m_ref, l_ref, acc_ref):
    m_ref[...] = jnp.full(m_ref.shape, NEG, F32)
    l_ref[...] = jnp.zeros(l_ref.shape, F32)
    acc_ref[...] = jnp.zeros(acc_ref.shape, F32)


def _norm_proj_body(x_ref, g_ref, w_ref, o_ref, h_ref):
    @pl.when(pl.program_id(1) == 0)
    def _():
        h_ref[...] = _rms(x_ref[...], g_ref[...]).astype(BF16)

    o_ref[...] = _dot(h_ref[...], w_ref[...]).astype(o_ref.dtype)


def _norm_proj_rope_body(x_ref, g_ref, w_ref, cos_ref, sa_ref, sb_ref, o_ref, h_ref, *, rope_tiles):
    j = pl.program_id(1)

    @pl.when(j == 0)
    def _():
        h_ref[...] = _rms(x_ref[...], g_ref[...]).astype(BF16)

    acc = _dot(h_ref[...], w_ref[...])
    is_rope = functools.reduce(jnp.logical_or, [j == t for t in rope_tiles])

    @pl.when(is_rope)
    def _():
        c, sa, sb = cos_ref[...], sa_ref[...], sb_ref[...]
        half = ROT_DIM // 2
        for hh in range(acc.shape[1] // HEAD_DIM):
            t = acc[:, hh * HEAD_DIM:(hh + 1) * HEAD_DIM]
            r = t * c + pltpu.roll(t, HEAD_DIM - half, 1) * sa + pltpu.roll(t, half, 1) * sb
            o_ref[:, hh * HEAD_DIM:(hh + 1) * HEAD_DIM] = r.astype(o_ref.dtype)

    @pl.when(jnp.logical_not(is_rope))
    def _():
        o_ref[...] = acc.astype(o_ref.dtype)


def norm_proj(x, g, w, *, tm, tn, out_dtype, rope=None, rope_tiles=()):
    T, D = x.shape
    N = w.shape[1]
    grid = (T // tm, N // tn)
    in_specs = [pl.BlockSpec((tm, D), lambda i, j: (i, 0)),
                pl.BlockSpec((1, D), lambda i, j: (0, 0)),
                pl.BlockSpec((D, tn), lambda i, j: (0, j))]
    args = [x, g.reshape(1, D), w]
    if rope is None:
        body = _norm_proj_body
    else:
        body = functools.partial(_norm_proj_rope_body, rope_tiles=rope_tiles)
        in_specs += [pl.BlockSpec((tm, HEAD_DIM), lambda i, j: (i, 0))] * 3
        args += list(rope)
    return pl.pallas_call(
        body,
        grid=grid,
        in_specs=in_specs,
        out_specs=pl.BlockSpec((tm, tn), lambda i, j: (i, j)),
        out_shape=jax.ShapeDtypeStruct((T, N), out_dtype),
        scratch_shapes=[pltpu.VMEM((tm, D), BF16)],
        compiler_params=_params(("parallel", "arbitrary"), 48),
        name="norm_proj",
    )(*args)


def _proj_norm_res_body(a_ref, w_ref, x_ref, g_ref, o_ref):
    y = _dot(a_ref[...], w_ref[...])
    o_ref[...] = x_ref[...] + _rms(y, g_ref[...])


def proj_norm_res(a, w, x, g, *, tm):
    T, K = a.shape
    D = w.shape[1]
    return pl.pallas_call(
        _proj_norm_res_body,
        grid=(T // tm,),
        in_specs=[pl.BlockSpec((tm, K), lambda i: (i, 0)),
                  pl.BlockSpec((K, D), lambda i: (0, 0)),
                  pl.BlockSpec((tm, D), lambda i: (i, 0)),
                  pl.BlockSpec((1, D), lambda i: (0, 0))],
        out_specs=pl.BlockSpec((tm, D), lambda i: (i, 0)),
        out_shape=jax.ShapeDtypeStruct((T, D), F32),
        compiler_params=_params(("parallel",), 48),
        name="proj_norm_res",
    )(a, w, x, g.reshape(1, D))


def _dilated_log_multiplicity(tq, n_back):
    r = np.arange(tq)[:, None]
    c = np.arange(tq)[None, :]
    out = []
    for j in range(n_back + 1):
        d = (n_back - j) * tq + r - c
        mult = np.zeros((tq, tq), np.float64)
        for window, dil in DILATED_CONFIGS:
            mult += (d >= 0) & (d % dil == 0) & (d <= window)
        out.append(np.where(mult > 0, np.log(np.maximum(mult, 1.0)), NEG))
    return np.stack(out).astype(np.float32)


def _dilated_body(q_ref, k_ref, v_ref, lb_ref, o_ref, m_ref, l_ref, acc_ref, *, n_back):
    i = pl.program_id(2)
    j = pl.program_id(3)

    @pl.when(j == 0)
    def _():
        _softmax_init(m_ref, l_ref, acc_ref)

    @pl.when(i - n_back + j >= 0)
    def _():
        s = _dot_nt(q_ref[...], k_ref[...]) + lb_ref[j]
        _softmax_step(s, v_ref[...], m_ref, l_ref, acc_ref)

    @pl.when(j == n_back)
    def _():
        o_ref[...] = (acc_ref[...] / l_ref[...]).astype(o_ref.dtype)


def dilated_attention(qkv, *, batch, seq, n_heads, q_col, k_col, v_col, tq):
    nq = seq // tq
    max_window = max(w for w, _ in DILATED_CONFIGS)
    n_back = -(-max_window // tq)
    lb = jnp.asarray(_dilated_log_multiplicity(tq, n_back))

    def kv_map(col):
        return lambda b, h, i, j: (b * nq + jnp.maximum(i - n_back + j, 0), col + h)

    return pl.pallas_call(
        functools.partial(_dilated_body, n_back=n_back),
        grid=(batch, n_heads, nq, n_back + 1),
        in_specs=[pl.BlockSpec((tq, HEAD_DIM), lambda b, h, i, j: (b * nq + i, q_col + h)),
                  pl.BlockSpec((tq, HEAD_DIM), kv_map(k_col)),
                  pl.BlockSpec((tq, HEAD_DIM), kv_map(v_col)),
                  pl.BlockSpec((n_back + 1, tq, tq), lambda b, h, i, j: (0, 0, 0))],
        out_specs=pl.BlockSpec((tq, HEAD_DIM), lambda b, h, i, j: (b * nq + i, h)),
        out_shape=jax.ShapeDtypeStruct((batch * seq, n_heads * HEAD_DIM), BF16),
        scratch_shapes=[pltpu.VMEM((tq, 1), F32), pltpu.VMEM((tq, 1), F32),
                        pltpu.VMEM((tq, HEAD_DIM), F32)],
        compiler_params=_params(("parallel", "parallel", "parallel", "arbitrary"), 32),
        name="dilated_attention",
    )(qkv, qkv, qkv, lb)


def _block_mean_body(k_ref, o_ref):
    n = pl.program_id(1)
    mean = jnp.mean(k_ref[...].astype(F32), axis=0, keepdims=True)
    o_ref[0, pl.ds(n, 1), :] = mean


def moba_block_means(qkv, *, batch, seq, n_heads, k_col):
    nblk = seq // MOBA_BLOCK
    width = n_heads * HEAD_DIM
    kc = k_col * HEAD_DIM // width
    return pl.pallas_call(
        _block_mean_body,
        grid=(batch, nblk),
        in_specs=[pl.BlockSpec((MOBA_BLOCK, width), lambda b, n: (b * nblk + n, kc))],
        out_specs=pl.BlockSpec((1, nblk, width), lambda b, n: (b, 0, 0)),
        out_shape=jax.ShapeDtypeStruct((batch, nblk, width), F32),
        compiler_params=_params(("parallel", "arbitrary"), 16),
        name="moba_block_means",
    )(qkv)


def _moba_body(q_ref, k_ref, v_ref, kbar_ref, o_ref, m_ref, l_ref, acc_ref, sel_ref, *, nblk):
    own = pl.program_id(2)
    bs = MOBA_BLOCK
    q = q_ref[...]

    kbar = kbar_ref[0]
    kbar = jnp.concatenate([kbar, jnp.zeros((LANES - nblk, HEAD_DIM), F32)], axis=0)
    kb_hi = kbar.astype(BF16)
    kb_lo = (kbar - kb_hi.astype(F32)).astype(BF16)
    gate = _dot_nt(q, kb_hi) + _dot_nt(q, kb_lo)

    lane = lax.broadcasted_iota(jnp.int32, gate.shape, 1)
    rank = jnp.zeros(gate.shape, jnp.int32)
    for m in range(nblk - 1):
        gm = gate[:, m:m + 1]
        beats = (gm > gate) | ((gm == gate) & (lane > m))
        rank = rank + jnp.where(beats & (own > m), 1, 0)
    selected = (lane < own) & (rank < MOBA_TOPK)
    sel_ref[...] = jnp.where(selected, 0.0, NEG)

    start = pl.multiple_of(own * bs, bs)
    row = lax.broadcasted_iota(jnp.int32, (bs, bs), 0)
    col = lax.broadcasted_iota(jnp.int32, (bs, bs), 1)
    s = jnp.where(col <= row, _dot_nt(q, k_ref[pl.ds(start, bs), :]), NEG)
    _softmax_init(m_ref, l_ref, acc_ref)
    _softmax_step(s, v_ref[pl.ds(start, bs), :], m_ref, l_ref, acc_ref)

    for n in range(nblk - 1):
        @pl.when(n < own)
        def _(n=n):
            s = _dot_nt(q, k_ref[n * bs:(n + 1) * bs, :]) + sel_ref[:, n:n + 1]
            _softmax_step(s, v_ref[n * bs:(n + 1) * bs, :], m_ref, l_ref, acc_ref)

    o_ref[...] = (acc_ref[...] / l_ref[...]).astype(o_ref.dtype)


def moba_attention(qkv, kbar, *, batch, seq, n_heads, q_col, k_col, v_col):
    bs = MOBA_BLOCK
    nblk = seq // bs
    return pl.pallas_call(
        functools.partial(_moba_body, nblk=nblk),
        grid=(batch, n_heads, nblk),
        in_specs=[pl.BlockSpec((bs, HEAD_DIM), lambda b, h, i: (b * nblk + i, q_col + h)),
                  pl.BlockSpec((seq, HEAD_DIM), lambda b, h, i: (b, k_col + h)),
                  pl.BlockSpec((seq, HEAD_DIM), lambda b, h, i: (b, v_col + h)),
                  pl.BlockSpec((1, nblk, HEAD_DIM), lambda b, h, i: (b, 0, h))],
        out_specs=pl.BlockSpec((bs, HEAD_DIM), lambda b, h, i: (b * nblk + i, h)),
        out_shape=jax.ShapeDtypeStruct((batch * seq, n_heads * HEAD_DIM), BF16),
        scratch_shapes=[pltpu.VMEM((bs, 1), F32), pltpu.VMEM((bs, 1), F32),
                        pltpu.VMEM((bs, HEAD_DIM), F32), pltpu.VMEM((bs, LANES), F32)],
        compiler_params=_params(("parallel", "parallel", "arbitrary"), 32),
        name="moba_attention",
    )(qkv, qkv, qkv, kbar)


def _fox_gate_body(x_ref, g_ref, wf_ref, bf_ref, tri_ref, c_ref, carry_ref):
    @pl.when(pl.program_id(1) == 0)
    def _():
        carry_ref[...] = jnp.zeros(carry_ref.shape, F32)

    h = _rms(x_ref[...], g_ref[...]).astype(BF16)
    fg = _dot(h, wf_ref[...]) + bf_ref[...]
    logf = jnp.minimum(fg, 0.0) - jnp.log(1.0 + jnp.exp(-jnp.abs(fg)))
    hi = logf.astype(BF16)
    rem = logf - hi.astype(F32)
    mid = rem.astype(BF16)
    lo = (rem - mid.astype(F32)).astype(BF16)
    tri = tri_ref[...]
    c = _dot(tri, hi) + _dot(tri, mid) + _dot(tri, lo) + carry_ref[0:1, :]
    c_ref[...] = c
    carry_ref[...] = jnp.broadcast_to(c[c.shape[0] - 1:, :], carry_ref.shape)


def fox_gate_cumsum(x, g, wf, bf, *, batch, seq, tm):
    T, D = x.shape
    ns = seq // tm
    tri = jnp.asarray(np.tril(np.ones((tm, tm), np.float32)), BF16)
    return pl.pallas_call(
        _fox_gate_body,
        grid=(batch, ns),
        in_specs=[pl.BlockSpec((tm, D), lambda b, s: (b * ns + s, 0)),
                  pl.BlockSpec((1, D), lambda b, s: (0, 0)),
                  pl.BlockSpec((D, LANES), lambda b, s: (0, 0)),
                  pl.BlockSpec((1, LANES), lambda b, s: (0, 0)),
                  pl.BlockSpec((tm, tm), lambda b, s: (0, 0))],
        out_specs=pl.BlockSpec((tm, LANES), lambda b, s: (b * ns + s, 0)),
        out_shape=jax.ShapeDtypeStruct((T, LANES), F32),
        scratch_shapes=[pltpu.VMEM((SUBLANES, LANES), F32)],
        compiler_params=_params(("parallel", "arbitrary"), 32),
        name="fox_gate_cumsum",
    )(x, g.reshape(1, D), wf, bf, tri)


def _fox_body(q_ref, k_ref, v_ref, ccol_ref, crow_ref, o_ref, m_ref, l_ref, acc_ref, cq_ref):
    h = pl.program_id(1)
    i = pl.program_id(2)
    j = pl.program_id(3)

    @pl.when(j == 0)
    def _():
        _softmax_init(m_ref, l_ref, acc_ref)
        cc = ccol_ref[...]
        lane = lax.broadcasted_iota(jnp.int32, cc.shape, 1)
        cq_ref[...] = jnp.sum(jnp.where(lane == h, cc, 0.0), axis=1, keepdims=True)

    def logits():
        return _dot_nt(q_ref[...], k_ref[...]) + cq_ref[...] - crow_ref[0]

    @pl.when(j < i)
    def _():
        _softmax_step(logits(), v_ref[...], m_ref, l_ref, acc_ref)

    @pl.when(j == i)
    def _():
        s = logits()
        row = lax.broadcasted_iota(jnp.int32, s.shape, 0)
        col = lax.broadcasted_iota(jnp.int32, s.shape, 1)
        _softmax_step(jnp.where(col <= row, s, NEG), v_ref[...], m_ref, l_ref, acc_ref)
        o_ref[...] = (acc_ref[...] / l_ref[...]).astype(o_ref.dtype)


def fox_attention(qkv, c_col, c_row, *, batch, seq, n_heads, tq):
    nq = seq // tq

    def kv_map(col):
        return lambda b, h, i, j: (b * nq + jnp.minimum(j, i), col + h)

    return pl.pallas_call(
        _fox_body,
        grid=(batch, n_heads, nq, nq),
        in_specs=[pl.BlockSpec((tq, HEAD_DIM), lambda b, h, i, j: (b * nq + i, h)),
                  pl.BlockSpec((tq, HEAD_DIM), kv_map(n_heads)),
                  pl.BlockSpec((tq, HEAD_DIM), kv_map(2 * n_heads)),
                  pl.BlockSpec((tq, LANES), lambda b, h, i, j: (b * nq + i, 0)),
                  pl.BlockSpec((1, 1, tq), lambda b, h, i, j: (b * n_heads + h, 0, jnp.minimum(j, i)))],
        out_specs=pl.BlockSpec((tq, HEAD_DIM), lambda b, h, i, j: (b * nq + i, h)),
        out_shape=jax.ShapeDtypeStruct((batch * seq, n_heads * HEAD_DIM), BF16),
        scratch_shapes=[pltpu.VMEM((tq, 1), F32), pltpu.VMEM((tq, 1), F32),
                        pltpu.VMEM((tq, HEAD_DIM), F32), pltpu.VMEM((tq, 1), F32)],
        compiler_params=_params(("parallel", "parallel", "parallel", "arbitrary"), 32),
        name="fox_attention",
    )(qkv, qkv, qkv, c_col, c_row)


def _mem_attn_body(x_ref, gpre_ref, wq_ref, kv_ref, wo_ref, gpost_ref, o_ref):
    x = x_ref[...]
    h = _rms(x, gpre_ref[...]).astype(BF16)
    q = _dot(h, wq_ref[...]).astype(BF16)
    width = MEM_HEADS * MEM_HEAD_DIM
    outs = []
    for hd in range(MEM_HEADS):
        lo, hi = hd * MEM_HEAD_DIM, (hd + 1) * MEM_HEAD_DIM
        s = _dot_nt(q[:, lo:hi], kv_ref[:, lo:hi])
        p = jnp.exp(s - jnp.max(s, axis=1, keepdims=True))
        o = _dot(p.astype(BF16), kv_ref[:, width + lo:width + hi])
        outs.append((o / jnp.sum(p, axis=1, keepdims=True)).astype(BF16))
    y = _dot(jnp.concatenate(outs, axis=1), wo_ref[...])
    o_ref[...] = x + _rms(y, gpost_ref[...])


def mem_attention(x, gpre, wq, kv, wo, gpost, *, seq, mem_tokens, tm):
    T, D = x.shape
    width = wq.shape[1]
    per_seq = seq // tm
    return pl.pallas_call(
        _mem_attn_body,
        grid=(T // tm,),
        in_specs=[pl.BlockSpec((tm, D), lambda i: (i, 0)),
                  pl.BlockSpec((1, D), lambda i: (0, 0)),
                  pl.BlockSpec((D, width), lambda i: (0, 0)),
                  pl.BlockSpec((mem_tokens, 2 * width), lambda i: (i // per_seq, 0)),
                  pl.BlockSpec((width, D), lambda i: (0, 0)),
                  pl.BlockSpec((1, D), lambda i: (0, 0))],
        out_specs=pl.BlockSpec((tm, D), lambda i: (i, 0)),
        out_shape=jax.ShapeDtypeStruct((T, D), F32),
        compiler_params=_params(("parallel",), 48),
        name="mem_attention",
    )(x, gpre.reshape(1, D), wq, kv, wo, gpost.reshape(1, D))


def _ffn_body(x_ref, gpre_ref, wg_ref, wv_ref, cwg_ref, cwv_ref, cbg_ref, cbv_ref, wd_ref, gpost_ref,
              o_ref, h_ref, acc_ref, ubuf_ref, carry_ref, *, tiles_per_seq):
    i = pl.program_id(0)
    f = pl.program_id(1)
    tm = x_ref.shape[0]
    halo = SUBLANES

    @pl.when(f == 0)
    def _():
        h_ref[...] = _rms(x_ref[...], gpre_ref[...]).astype(BF16)
        acc_ref[...] = jnp.zeros(acc_ref.shape, F32)

    h = h_ref[...]
    seq_start = i % tiles_per_seq == 0

    def conv_branch(w_ref, cw_ref, cb_ref, slot):
        u = _dot(h, w_ref[...])
        ubuf_ref[slot, 0:halo, :] = jnp.where(seq_start, 0.0, carry_ref[f, slot])
        ubuf_ref[slot, halo:, :] = u
        carry_ref[f, slot] = u[tm - halo:, :]
        cw = cw_ref[...]
        return (cb_ref[...] + cw[0:1] * ubuf_ref[slot, halo - 2:halo - 2 + tm, :]
                + cw[1:2] * ubuf_ref[slot, halo - 1:halo - 1 + tm, :] + cw[2:3] * u)

    g = conv_branch(wg_ref, cwg_ref, cbg_ref, 0)
    v = conv_branch(wv_ref, cwv_ref, cbv_ref, 1)
    a = (g / (1.0 + jnp.exp(-g)) * v).astype(BF16)
    acc_ref[...] += _dot(a, wd_ref[...])

    @pl.when(f == pl.num_programs(1) - 1)
    def _():
        o_ref[...] = x_ref[...] + _rms(acc_ref[...], gpost_ref[...])


def conv_ffn(x, gpre, w_up, conv_w, conv_b, w_down, gpost, *, seq, tm, tf):
    T, D = x.shape
    dff = w_down.shape[0]
    nf = dff // tf
    cb = conv_b.reshape(1, 2 * dff)
    return pl.pallas_call(
        functools.partial(_ffn_body, tiles_per_seq=seq // tm),
        grid=(T // tm, nf),
        in_specs=[pl.BlockSpec((tm, D), lambda i, f: (i, 0)),
                  pl.BlockSpec((1, D), lambda i, f: (0, 0)),
                  pl.BlockSpec((D, tf), lambda i, f: (0, f)),
                  pl.BlockSpec((D, tf), lambda i, f: (0, f + nf)),
                  pl.BlockSpec((CONV_WIDTH, tf), lambda i, f: (0, f)),
                  pl.BlockSpec((CONV_WIDTH, tf), lambda i, f: (0, f + nf)),
                  pl.BlockSpec((1, tf), lambda i, f: (0, f)),
                  pl.BlockSpec((1, tf), lambda i, f: (0, f + nf)),
                  pl.BlockSpec((tf, D), lambda i, f: (f, 0)),
                  pl.BlockSpec((1, D), lambda i, f: (0, 0))],
        out_specs=pl.BlockSpec((tm, D), lambda i, f: (i, 0)),
        out_shape=jax.ShapeDtypeStruct((T, D), F32),
        scratch_shapes=[pltpu.VMEM((tm, D), BF16),
                        pltpu.VMEM((tm, D), F32),
                        pltpu.VMEM((2, SUBLANES + tm, tf), F32),
                        pltpu.VMEM((nf, 2, SUBLANES, tf), F32)],
        compiler_params=_params(("arbitrary", "arbitrary"), 56),
        name="conv_ffn",
    )(x, gpre.reshape(1, D), w_up, w_up, conv_w, conv_w, cb, cb, w_down, gpost.reshape(1, D))


def _rope_tables(positions):
    half = ROT_DIM // 2
    inv_freq = ROPE_THETA ** (-jnp.arange(0, ROT_DIM, 2, dtype=F32) / ROT_DIM)
    ang = positions.astype(F32).reshape(-1, 1) * inv_freq
    cos, sin = jnp.cos(ang), jnp.sin(ang)
    T = ang.shape[0]
    ones = jnp.ones((T, HEAD_DIM - ROT_DIM), F32)
    zeros = jnp.zeros((T, HEAD_DIM - half), F32)
    c = jnp.concatenate([cos, cos, ones], axis=1)
    sa = jnp.concatenate([-sin, zeros], axis=1)
    sb = jnp.concatenate([jnp.zeros((T, half), F32), sin, zeros[:, half:]], axis=1)
    return c, sa, sb


def _even_mixer(x, g_pre, w_in, w_out, g_post, rope, *, batch, seq):
    scale = HEAD_DIM ** -0.5
    part = N_HEADS_A * HEAD_DIM
    col_scale = jnp.ones((6, 1), F32).at[0].set(scale).at[3].set(scale)
    col_scale = jnp.broadcast_to(col_scale, (6, part)).reshape(1, 6 * part)
    w = (w_in * col_scale).astype(BF16)
    qkv = norm_proj(x, g_pre, w, tm=512, tn=part, out_dtype=BF16, rope=rope, rope_tiles=(0, 1, 3, 4))
    hb = N_HEADS_A
    oa = dilated_attention(qkv, batch=batch, seq=seq, n_heads=N_HEADS_A,
                           q_col=0, k_col=hb, v_col=2 * hb, tq=512)
    kbar = moba_block_means(qkv, batch=batch, seq=seq, n_heads=N_HEADS_B, k_col=4 * hb)
    ob = moba_attention(qkv, kbar, batch=batch, seq=seq, n_heads=N_HEADS_B,
                        q_col=3 * hb, k_col=4 * hb, v_col=5 * hb)
    o = jnp.concatenate([oa, ob], axis=1)
    return proj_norm_res(o, w_out.astype(BF16), x, g_post, tm=512)


def _odd_mixer(x, g_pre, w_in, b_f, w_out, g_post, *, batch, seq):
    scale = HEAD_DIM ** -0.5
    nqkv = 3 * D_MODEL
    col_scale = jnp.concatenate([jnp.full((1, D_MODEL), scale, F32), jnp.ones((1, 2 * D_MODEL), F32)], axis=1)
    w = (w_in[:, :nqkv] * col_scale).astype(BF16)
    qkv = norm_proj(x, g_pre, w, tm=512, tn=1024, out_dtype=BF16)
    wf = jnp.pad(w_in[:, nqkv:], ((0, 0), (0, LANES - N_HEADS))).astype(BF16)
    bf = jnp.pad(b_f.astype(F32), (0, LANES - N_HEADS)).reshape(1, LANES)
    c_col = fox_gate_cumsum(x, g_pre, wf, bf, batch=batch, seq=seq, tm=512)
    c_row = c_col[:, :N_HEADS].reshape(batch, seq, N_HEADS).transpose(0, 2, 1).reshape(batch * N_HEADS, 1, seq)
    o = fox_attention(qkv, c_col, c_row, batch=batch, seq=seq, n_heads=N_HEADS, tq=512)
    return proj_norm_res(o, w_out.astype(BF16), x, g_post, tm=512)


def kernel(x, mem, positions, w_in_ab, w_out_ab, w_in_c, b_f, w_out_c, g_mix_pre, g_mix_post, g_mem_kv, g_mem_pre, g_mem_post, w_mq, w_mk, w_mv, w_mo, g_ffn_pre, g_ffn_post, w_up, conv_w, conv_b, w_down):
    batch, seq, d = x.shape
    mem_tokens = mem.shape[1]
    depth = g_mix_pre.shape[0]
    rope = _rope_tables(positions)
    xs = x.reshape(batch * seq, d)
    mems = mem.reshape(batch * mem_tokens, d)
    mem_scale = MEM_HEAD_DIM ** -0.5
    for layer in range(depth):
        if layer % 2 == 0:
            xs = _even_mixer(xs, g_mix_pre[layer], w_in_ab[layer // 2], w_out_ab[layer // 2],
                             g_mix_post[layer], rope, batch=batch, seq=seq)
        else:
            xs = _odd_mixer(xs, g_mix_pre[layer], w_in_c[layer // 2], b_f[layer // 2], w_out_c[layer // 2],
                            g_mix_post[layer], batch=batch, seq=seq)
        w_kv = jnp.concatenate([w_mk[layer], w_mv[layer]], axis=1).astype(BF16)
        kv = norm_proj(mems, g_mem_kv[layer], w_kv, tm=512, tn=w_kv.shape[1], out_dtype=BF16)
        xs = mem_attention(xs, g_mem_pre[layer], (w_mq[layer] * mem_scale).astype(BF16), kv,
                           w_mo[layer].astype(BF16), g_mem_post[layer],
                           seq=seq, mem_tokens=mem_tokens, tm=512)
        xs = conv_ffn(xs, g_ffn_pre[layer], w_up[layer].astype(BF16), conv_w[layer], conv_b[layer],
                      w_down[layer].astype(BF16), g_ffn_post[layer], seq=seq, tm=512, tf=512)
    return xs.reshape(batch, seq, d)
```

```python
import functools

import numpy as np
import jax
import jax.numpy as jnp
from jax import lax
from jax.experimental import pallas as pl
from jax.experimental.pallas import tpu as pltpu

D_MODEL = 2048
HEAD_DIM = 128
N_HEADS = D_MODEL // HEAD_DIM
N_HEADS_A = N_HEADS // 2
N_HEADS_B = N_HEADS - N_HEADS_A
DILATED_CONFIGS = ((128, 1), (512, 4), (2048, 16))
MOBA_BLOCK = 256
MOBA_TOPK = 3
ROT_DIM = HEAD_DIM // 4
ROPE_THETA = 500000.0
MEM_HEADS = 4
MEM_HEAD_DIM = 128
D_FF = ((8 * D_MODEL // 3 + 255) // 256) * 256
CONV_WIDTH = 3
RMS_EPS = 1e-6

LANES = 128
SUBLANES = 8
NEG = -1e30
HEADS_PER_STEP = 2
F32 = jnp.float32
BF16 = jnp.bfloat16
MIB = 1024 * 1024


def _params(semantics, vmem_mib):
    return pltpu.CompilerParams(dimension_semantics=semantics,
                                vmem_limit_bytes=vmem_mib * MIB)


def _rms(x, g):
    ms = jnp.mean(x * x, axis=-1, keepdims=True)
    return x * lax.rsqrt(ms + RMS_EPS) * g


def _dot(a, b):
    return jnp.dot(a, b, preferred_element_type=F32)


def _dot_nt(a, b):
    return lax.dot_general(a, b, (((1,), (1,)), ((), ())), preferred_element_type=F32)


def _softmax_init(m_ref, l_ref, acc_ref):
    m_ref[...] = jnp.full(m_ref.shape, NEG, F32)
    l_ref[...] = jnp.zeros(l_ref.shape, F32)
    acc_ref[...] = jnp.zeros(acc_ref.shape, F32)


def _multi_softmax_step(s_list, v_list, m_refs, l_refs, acc_refs):
    m_prev = [r[...] for r in m_refs]
    m_new = [jnp.maximum(mp, jnp.max(s, axis=1, keepdims=True)) for mp, s in zip(m_prev, s_list)]
    p_list = [jnp.exp(s - _tile_lanes(mn, s.shape[1])) for s, mn in zip(s_list, m_new)]
    for a, p in enumerate(p_list):
        alpha = jnp.exp(m_prev[a] - m_new[a])
        l_refs[a][...] = alpha * l_refs[a][...] + jnp.sum(p, axis=1, keepdims=True)
        acc_refs[a][...] = alpha * acc_refs[a][...] + _dot(p.astype(BF16), v_list[a])
        m_refs[a][...] = m_new[a]


def _tile_lanes(x, width):
    n = width // LANES
    return x if n == 1 else jnp.concatenate([x] * n, axis=1)


def _softmax_scratch(tq, nh):
    return ([pltpu.VMEM((tq, LANES), F32)] * (2 * nh) + [pltpu.VMEM((tq, HEAD_DIM), F32)] * nh)


def _head_cols(a):
    return slice(a * HEAD_DIM, (a + 1) * HEAD_DIM)


def _norm_proj_body(x_ref, g_ref, w_ref, o_ref, h_ref):
    @pl.when(pl.program_id(1) == 0)
    def _():
        h_ref[...] = _rms(x_ref[...], g_ref[...]).astype(BF16)

    o_ref[...] = _dot(h_ref[...], w_ref[...]).astype(o_ref.dtype)


def _norm_proj_rope_body(x_ref, g_ref, w_ref, cos_ref, sa_ref, sb_ref, o_ref, h_ref, *, rope_tiles):
    j = pl.program_id(1)

    @pl.when(j == 0)
    def _():
        h_ref[...] = _rms(x_ref[...], g_ref[...]).astype(BF16)

    acc = _dot(h_ref[...], w_ref[...])
    is_rope = functools.reduce(jnp.logical_or, [j == t for t in rope_tiles])

    @pl.when(is_rope)
    def _():
        c, sa, sb = cos_ref[...], sa_ref[...], sb_ref[...]
        half = ROT_DIM // 2
        for hh in range(acc.shape[1] // HEAD_DIM):
            t = acc[:, hh * HEAD_DIM:(hh + 1) * HEAD_DIM]
            r = t * c + pltpu.roll(t, HEAD_DIM - half, 1) * sa + pltpu.roll(t, half, 1) * sb
            o_ref[:, hh * HEAD_DIM:(hh + 1) * HEAD_DIM] = r.astype(o_ref.dtype)

    @pl.when(jnp.logical_not(is_rope))
    def _():
        o_ref[...] = acc.astype(o_ref.dtype)


def norm_proj(x, g, w, *, tm, tn, out_dtype, rope=None, rope_tiles=()):
    T, D = x.shape
    N = w.shape[1]
    grid = (T // tm, N // tn)
    in_specs = [pl.BlockSpec((tm, D), lambda i, j: (i, 0)),
                pl.BlockSpec((1, D), lambda i, j: (0, 0)),
                pl.BlockSpec((D, tn), lambda i, j: (0, j))]
    args = [x, g.reshape(1, D), w]
    if rope is None:
        body = _norm_proj_body
    else:
        body = functools.partial(_norm_proj_rope_body, rope_tiles=rope_tiles)
        in_specs += [pl.BlockSpec((tm, HEAD_DIM), lambda i, j: (i, 0))] * 3
        args += list(rope)
    return pl.pallas_call(
        body,
        grid=grid,
        in_specs=in_specs,
        out_specs=pl.BlockSpec((tm, tn), lambda i, j: (i, j)),
        out_shape=jax.ShapeDtypeStruct((T, N), out_dtype),
        scratch_shapes=[pltpu.VMEM((tm, D), BF16)],
        compiler_params=_params(("parallel", "arbitrary"), 48),
        name="norm_proj",
    )(*args)


def _proj_norm_res_body(a_ref, w_ref, x_ref, g_ref, o_ref):
    y = _dot(a_ref[...], w_ref[...])
    o_ref[...] = x_ref[...] + _rms(y, g_ref[...])


def proj_norm_res(a, w, x, g, *, tm):
    T, K = a.shape
    D = w.shape[1]
    return pl.pallas_call(
        _proj_norm_res_body,
        grid=(T // tm,),
        in_specs=[pl.BlockSpec((tm, K), lambda i: (i, 0)),
                  pl.BlockSpec((K, D), lambda i: (0, 0)),
                  pl.BlockSpec((tm, D), lambda i: (i, 0)),
                  pl.BlockSpec((1, D), lambda i: (0, 0))],
        out_specs=pl.BlockSpec((tm, D), lambda i: (i, 0)),
        out_shape=jax.ShapeDtypeStruct((T, D), F32),
        compiler_params=_params(("parallel",), 48),
        name="proj_norm_res",
    )(a, w, x, g.reshape(1, D))


def _dilated_log_multiplicity(tq, n_back):
    r = np.arange(tq)[:, None]
    c = np.arange(tq)[None, :]
    out = []
    for j in range(n_back + 1):
        d = (n_back - j) * tq + r - c
        mult = np.zeros((tq, tq), np.float64)
        for window, dil in DILATED_CONFIGS:
            mult += (d >= 0) & (d % dil == 0) & (d <= window)
        out.append(np.where(mult > 0, np.log(np.maximum(mult, 1.0)), NEG))
    return np.stack(out).astype(np.float32)


def _dilated_body(q_ref, k_ref, v_ref, lb_ref, o_ref, *scratch, n_back):
    nh = HEADS_PER_STEP
    m_refs, l_refs, acc_refs = (scratch[n * nh:(n + 1) * nh] for n in range(3))
    i = pl.program_id(2)
    tq = q_ref.shape[0]
    for a in range(nh):
        _softmax_init(m_refs[a], l_refs[a], acc_refs[a])

    def kv_tile(j, carry):
        start = pl.multiple_of((i - n_back + j) * tq, tq)
        lb = lb_ref[j]
        s_list = [_dot_nt(q_ref[:, _head_cols(a)], k_ref[pl.ds(start, tq), _head_cols(a)]) + lb
                  for a in range(nh)]
        v_list = [v_ref[pl.ds(start, tq), _head_cols(a)] for a in range(nh)]
        _multi_softmax_step(s_list, v_list, m_refs, l_refs, acc_refs)
        return carry

    lax.fori_loop(jnp.maximum(n_back - i, 0), n_back + 1, kv_tile, 0)
    for a in range(nh):
        o_ref[:, _head_cols(a)] = (acc_refs[a][...] / l_refs[a][...]).astype(o_ref.dtype)


def dilated_attention(qkv, *, batch, seq, n_heads, q_col, k_col, v_col, tq):
    nq = seq // tq
    nh = HEADS_PER_STEP
    groups = n_heads // nh
    width = nh * HEAD_DIM
    max_window = max(w for w, _ in DILATED_CONFIGS)
    n_back = -(-max_window // tq)
    lb = jnp.asarray(_dilated_log_multiplicity(tq, n_back))
    qg, kg, vg = q_col // nh, k_col // nh, v_col // nh
    return pl.pallas_call(
        functools.partial(_dilated_body, n_back=n_back),
        grid=(batch, groups, nq),
        in_specs=[pl.BlockSpec((tq, width), lambda b, g, i: (b * nq + i, qg + g)),
                  pl.BlockSpec((seq, width), lambda b, g, i: (b, kg + g)),
                  pl.BlockSpec((seq, width), lambda b, g, i: (b, vg + g)),
                  pl.BlockSpec((n_back + 1, tq, tq), lambda b, g, i: (0, 0, 0))],
        out_specs=pl.BlockSpec((tq, width), lambda b, g, i: (b * nq + i, g)),
        out_shape=jax.ShapeDtypeStruct((batch * seq, n_heads * HEAD_DIM), BF16),
        scratch_shapes=_softmax_scratch(tq, nh),
        compiler_params=_params(("parallel", "parallel", "arbitrary"), 40),
        name="dilated_attention",
    )(qkv, qkv, qkv, lb)


def _block_mean_body(k_ref, o_ref):
    n = pl.program_id(1)
    mean = jnp.mean(k_ref[...].astype(F32), axis=0, keepdims=True)
    o_ref[0, pl.ds(n, 1), :] = mean


def moba_block_means(qkv, *, batch, seq, n_heads, k_col):
    nblk = seq // MOBA_BLOCK
    width = n_heads * HEAD_DIM
    kc = k_col * HEAD_DIM // width
    return pl.pallas_call(
        _block_mean_body,
        grid=(batch, nblk),
        in_specs=[pl.BlockSpec((MOBA_BLOCK, width), lambda b, n: (b * nblk + n, kc))],
        out_specs=pl.BlockSpec((1, nblk, width), lambda b, n: (b, 0, 0)),
        out_shape=jax.ShapeDtypeStruct((batch, nblk, width), F32),
        compiler_params=_params(("parallel", "arbitrary"), 16),
        name="moba_block_means",
    )(qkv)


def _moba_select_bias(q, kbar, own, nblk):
    kbar = jnp.concatenate([kbar, jnp.zeros((LANES - nblk, HEAD_DIM), F32)], axis=0)
    kb_hi = kbar.astype(BF16)
    kb_lo = (kbar - kb_hi.astype(F32)).astype(BF16)
    gate = _dot_nt(q, kb_hi) + _dot_nt(q, kb_lo)
    lane = lax.broadcasted_iota(jnp.int32, gate.shape, 1)
    lane_f = lane.astype(F32)
    past = lane < own
    g = jnp.where(past, gate, -jnp.inf)
    selected = jnp.zeros(gate.shape, jnp.bool_)
    for _ in range(MOBA_TOPK):
        mx = jnp.max(g, axis=1, keepdims=True)
        first = jnp.min(jnp.where(g == mx, lane_f, float(LANES)), axis=1, keepdims=True)
        pick = (lane_f == first) & past
        selected = selected | pick
        g = jnp.where(pick, -jnp.inf, g)
    return jnp.where(selected, 0.0, NEG)


def _moba_body(q_ref, k_ref, v_ref, kbar_ref, o_ref, *scratch, nblk):
    nh = HEADS_PER_STEP
    m_refs, l_refs, acc_refs, sel_refs = (scratch[n * nh:(n + 1) * nh] for n in range(4))
    own = pl.program_id(2)
    bs = MOBA_BLOCK

    for a in range(nh):
        _softmax_init(m_refs[a], l_refs[a], acc_refs[a])
        sel_refs[a][...] = _moba_select_bias(q_ref[:, _head_cols(a)], kbar_ref[0, :, _head_cols(a)], own, nblk)

    def tiles(start):
        s_list = [_dot_nt(q_ref[:, _head_cols(a)], k_ref[pl.ds(start, bs), _head_cols(a)]) for a in range(nh)]
        v_list = [v_ref[pl.ds(start, bs), _head_cols(a)] for a in range(nh)]
        return s_list, v_list

    s_list, v_list = tiles(pl.multiple_of(own * bs, bs))
    row = lax.broadcasted_iota(jnp.int32, (bs, bs), 0)
    col = lax.broadcasted_iota(jnp.int32, (bs, bs), 1)
    s_list = [jnp.where(col <= row, s, NEG) for s in s_list]
    _multi_softmax_step(s_list, v_list, m_refs, l_refs, acc_refs)

    lane = lax.broadcasted_iota(jnp.int32, (bs, LANES), 1)

    def past_block(n, carry):
        s_list, v_list = tiles(pl.multiple_of(n * bs, bs))
        bias = [jnp.sum(jnp.where(lane == n, sel_refs[a][...], 0.0), axis=1, keepdims=True) for a in range(nh)]
        s_list = [s + b for s, b in zip(s_list, bias)]
        _multi_softmax_step(s_list, v_list, m_refs, l_refs, acc_refs)
        return carry

    lax.fori_loop(0, own, past_block, 0)
    for a in range(nh):
        o_ref[:, _head_cols(a)] = (acc_refs[a][...] / l_refs[a][...]).astype(o_ref.dtype)


def moba_attention(qkv, kbar, *, batch, seq, n_heads, q_col, k_col, v_col):
    bs = MOBA_BLOCK
    nblk = seq // bs
    nh = HEADS_PER_STEP
    groups = n_heads // nh
    width = nh * HEAD_DIM
    qg, kg, vg = q_col // nh, k_col // nh, v_col // nh
    return pl.pallas_call(
        functools.partial(_moba_body, nblk=nblk),
        grid=(batch, groups, nblk),
        in_specs=[pl.BlockSpec((bs, width), lambda b, g, i: (b * nblk + i, qg + g)),
                  pl.BlockSpec((seq, width), lambda b, g, i: (b, kg + g)),
                  pl.BlockSpec((seq, width), lambda b, g, i: (b, vg + g)),
                  pl.BlockSpec((1, nblk, width), lambda b, g, i: (b, 0, g))],
        out_specs=pl.BlockSpec((bs, width), lambda b, g, i: (b * nblk + i, g)),
        out_shape=jax.ShapeDtypeStruct((batch * seq, n_heads * HEAD_DIM), BF16),
        scratch_shapes=_softmax_scratch(bs, nh) + [pltpu.VMEM((bs, LANES), F32)] * nh,
        compiler_params=_params(("parallel", "parallel", "arbitrary"), 32),
        name="moba_attention",
    )(qkv, qkv, qkv, kbar)


def _fox_gate_body(x_ref, g_ref, wf_ref, bf_ref, tri_ref, c_ref, carry_ref):
    @pl.when(pl.program_id(1) == 0)
    def _():
        carry_ref[...] = jnp.zeros(carry_ref.shape, F32)

    h = _rms(x_ref[...], g_ref[...]).astype(BF16)
    fg = _dot(h, wf_ref[...]) + bf_ref[...]
    logf = jnp.minimum(fg, 0.0) - jnp.log(1.0 + jnp.exp(-jnp.abs(fg)))
    hi = logf.astype(BF16)
    rem = logf - hi.astype(F32)
    mid = rem.astype(BF16)
    lo = (rem - mid.astype(F32)).astype(BF16)
    tri = tri_ref[...]
    c = _dot(tri, hi) + _dot(tri, mid) + _dot(tri, lo) + carry_ref[0:1, :]
    c_ref[...] = c
    carry_ref[...] = jnp.broadcast_to(c[c.shape[0] - 1:, :], carry_ref.shape)


def fox_gate_cumsum(x, g, wf, bf, *, batch, seq, tm):
    T, D = x.shape
    ns = seq // tm
    tri = jnp.asarray(np.tril(np.ones((tm, tm), np.float32)), BF16)
    return pl.pallas_call(
        _fox_gate_body,
        grid=(batch, ns),
        in_specs=[pl.BlockSpec((tm, D), lambda b, s: (b * ns + s, 0)),
                  pl.BlockSpec((1, D), lambda b, s: (0, 0)),
                  pl.BlockSpec((D, LANES), lambda b, s: (0, 0)),
                  pl.BlockSpec((1, LANES), lambda b, s: (0, 0)),
                  pl.BlockSpec((tm, tm), lambda b, s: (0, 0))],
        out_specs=pl.BlockSpec((tm, LANES), lambda b, s: (b * ns + s, 0)),
        out_shape=jax.ShapeDtypeStruct((T, LANES), F32),
        scratch_shapes=[pltpu.VMEM((SUBLANES, LANES), F32)],
        compiler_params=_params(("parallel", "arbitrary"), 32),
        name="fox_gate_cumsum",
    )(x, g.reshape(1, D), wf, bf, tri)


def _fox_body(q_ref, k_ref, v_ref, ccol_ref, crow_ref, o_ref, *scratch):
    nh = HEADS_PER_STEP
    m_refs, l_refs, acc_refs, cq_refs = (scratch[n * nh:(n + 1) * nh] for n in range(4))
    hp = pl.program_id(1)
    i = pl.program_id(2)
    tq = q_ref.shape[0]

    cc = ccol_ref[...]
    lane = lax.broadcasted_iota(jnp.int32, cc.shape, 1)
    for a in range(nh):
        _softmax_init(m_refs[a], l_refs[a], acc_refs[a])
        cq = jnp.sum(jnp.where(lane == hp * nh + a, cc, 0.0), axis=1, keepdims=True)
        cq_refs[a][...] = jnp.broadcast_to(cq, cq_refs[a].shape)

    def tiles(j):
        start = pl.multiple_of(j * tq, tq)
        s_list, v_list = [], []
        for a in range(nh):
            s = _dot_nt(q_ref[:, _head_cols(a)], k_ref[pl.ds(start, tq), _head_cols(a)])
            s_list.append(s + _tile_lanes(cq_refs[a][...], tq) - crow_ref[a, pl.ds(j, 1), :])
            v_list.append(v_ref[pl.ds(start, tq), _head_cols(a)])
        return s_list, v_list

    def below_diagonal(j, carry):
        s_list, v_list = tiles(j)
        _multi_softmax_step(s_list, v_list, m_refs, l_refs, acc_refs)
        return carry

    lax.fori_loop(0, i, below_diagonal, 0)

    s_list, v_list = tiles(i)
    row = lax.broadcasted_iota(jnp.int32, (tq, tq), 0)
    col = lax.broadcasted_iota(jnp.int32, (tq, tq), 1)
    s_list = [jnp.where(col <= row, s, NEG) for s in s_list]
    _multi_softmax_step(s_list, v_list, m_refs, l_refs, acc_refs)
    for a in range(nh):
        o_ref[:, _head_cols(a)] = (acc_refs[a][...] / l_refs[a][...]).astype(o_ref.dtype)


def fox_attention(qkv, c_col, c_row, *, batch, seq, n_heads, tq):
    nq = seq // tq
    nh = HEADS_PER_STEP
    groups = n_heads // nh
    width = nh * HEAD_DIM
    return pl.pallas_call(
        _fox_body,
        grid=(batch, groups, nq),
        in_specs=[pl.BlockSpec((tq, width), lambda b, g, i: (b * nq + i, g)),
                  pl.BlockSpec((seq, width), lambda b, g, i: (b, groups + g)),
                  pl.BlockSpec((seq, width), lambda b, g, i: (b, 2 * groups + g)),
                  pl.BlockSpec((tq, LANES), lambda b, g, i: (b * nq + i, 0)),
                  pl.BlockSpec((nh, nq, tq), lambda b, g, i: (b * groups + g, 0, 0))],
        out_specs=pl.BlockSpec((tq, width), lambda b, g, i: (b * nq + i, g)),
        out_shape=jax.ShapeDtypeStruct((batch * seq, n_heads * HEAD_DIM), BF16),
        scratch_shapes=_softmax_scratch(tq, nh) + [pltpu.VMEM((tq, LANES), F32)] * nh,
        compiler_params=_params(("parallel", "parallel", "arbitrary"), 40),
        name="fox_attention",
    )(qkv, qkv, qkv, c_col, c_row)


def _mem_attn_body(x_ref, gpre_ref, wq_ref, kv_ref, wo_ref, gpost_ref, o_ref):
    x = x_ref[...]
    h = _rms(x, gpre_ref[...]).astype(BF16)
    q = _dot(h, wq_ref[...]).astype(BF16)
    width = MEM_HEADS * MEM_HEAD_DIM
    outs = []
    for hd in range(MEM_HEADS):
        lo, hi = hd * MEM_HEAD_DIM, (hd + 1) * MEM_HEAD_DIM
        s = _dot_nt(q[:, lo:hi], kv_ref[:, lo:hi])
        p = jnp.exp(s - jnp.max(s, axis=1, keepdims=True))
        o = _dot(p.astype(BF16), kv_ref[:, width + lo:width + hi])
        outs.append((o / jnp.sum(p, axis=1, keepdims=True)).astype(BF16))
    y = _dot(jnp.concatenate(outs, axis=1), wo_ref[...])
    o_ref[...] = x + _rms(y, gpost_ref[...])


def mem_attention(x, gpre, wq, kv, wo, gpost, *, seq, mem_tokens, tm):
    T, D = x.shape
    width = wq.shape[1]
    per_seq = seq // tm
    return pl.pallas_call(
        _mem_attn_body,
        grid=(T // tm,),
        in_specs=[pl.BlockSpec((tm, D), lambda i: (i, 0)),
                  pl.BlockSpec((1, D), lambda i: (0, 0)),
                  pl.BlockSpec((D, width), lambda i: (0, 0)),
                  pl.BlockSpec((mem_tokens, 2 * width), lambda i: (i // per_seq, 0)),
                  pl.BlockSpec((width, D), lambda i: (0, 0)),
                  pl.BlockSpec((1, D), lambda i: (0, 0))],
        out_specs=pl.BlockSpec((tm, D), lambda i: (i, 0)),
        out_shape=jax.ShapeDtypeStruct((T, D), F32),
        compiler_params=_params(("parallel",), 48),
        name="mem_attention",
    )(x, gpre.reshape(1, D), wq, kv, wo, gpost.reshape(1, D))


def _ffn_body(x_ref, gpre_ref, wg_ref, wv_ref, cwg_ref, cwv_ref, cbg_ref, cbv_ref, wd_ref, gpost_ref,
              o_ref, h_ref, acc_ref, ubuf_ref, carry_ref, *, tiles_per_seq):
    i = pl.program_id(0)
    f = pl.program_id(1)
    tm = x_ref.shape[0]
    halo = SUBLANES

    @pl.when(f == 0)
    def _():
        h_ref[...] = _rms(x_ref[...], gpre_ref[...]).astype(BF16)
        acc_ref[...] = jnp.zeros(acc_ref.shape, F32)

    h = h_ref[...]
    seq_start = i % tiles_per_seq == 0

    def conv_branch(w_ref, cw_ref, cb_ref, slot):
        u = _dot(h, w_ref[...])
        ubuf_ref[slot, 0:halo, :] = jnp.where(seq_start, 0.0, carry_ref[f, slot])
        ubuf_ref[slot, halo:, :] = u
        carry_ref[f, slot] = u[tm - halo:, :]
        cw = cw_ref[...]
        return (cb_ref[...] + cw[0:1] * ubuf_ref[slot, halo - 2:halo - 2 + tm, :]
                + cw[1:2] * ubuf_ref[slot, halo - 1:halo - 1 + tm, :] + cw[2:3] * u)

    g = conv_branch(wg_ref, cwg_ref, cbg_ref, 0)
    v = conv_branch(wv_ref, cwv_ref, cbv_ref, 1)
    a = (g / (1.0 + jnp.exp(-g)) * v).astype(BF16)
    acc_ref[...] += _dot(a, wd_ref[...])

    @pl.when(f == pl.num_programs(1) - 1)
    def _():
        o_ref[...] = x_ref[...] + _rms(acc_ref[...], gpost_ref[...])


def conv_ffn(x, gpre, w_up, conv_w, conv_b, w_down, gpost, *, seq, tm, tf):
    T, D = x.shape
    dff = w_down.shape[0]
    nf = dff // tf
    cb = conv_b.reshape(1, 2 * dff)
    return pl.pallas_call(
        functools.partial(_ffn_body, tiles_per_seq=seq // tm),
        grid=(T // tm, nf),
        in_specs=[pl.BlockSpec((tm, D), lambda i, f: (i, 0)),
                  pl.BlockSpec((1, D), lambda i, f: (0, 0)),
                  pl.BlockSpec((D, tf), lambda i, f: (0, f)),
                  pl.BlockSpec((D, tf), lambda i, f: (0, f + nf)),
                  pl.BlockSpec((CONV_WIDTH, tf), lambda i, f: (0, f)),
                  pl.BlockSpec((CONV_WIDTH, tf), lambda i, f: (0, f + nf)),
                  pl.BlockSpec((1, tf), lambda i, f: (0, f)),
                  pl.BlockSpec((1, tf), lambda i, f: (0, f + nf)),
                  pl.BlockSpec((tf, D), lambda i, f: (f, 0)),
                  pl.BlockSpec((1, D), lambda i, f: (0, 0))],
        out_specs=pl.BlockSpec((tm, D), lambda i, f: (i, 0)),
        out_shape=jax.ShapeDtypeStruct((T, D), F32),
        scratch_shapes=[pltpu.VMEM((tm, D), BF16),
                        pltpu.VMEM((tm, D), F32),
                        pltpu.VMEM((2, SUBLANES + tm, tf), F32),
                        pltpu.VMEM((nf, 2, SUBLANES, tf), F32)],
        compiler_params=_params(("arbitrary", "arbitrary"), 56),
        name="conv_ffn",
    )(x, gpre.reshape(1, D), w_up, w_up, conv_w, conv_w, cb, cb, w_down, gpost.reshape(1, D))


def _rope_tables(positions):
    half = ROT_DIM // 2
    inv_freq = ROPE_THETA ** (-jnp.arange(0, ROT_DIM, 2, dtype=F32) / ROT_DIM)
    ang = positions.astype(F32).reshape(-1, 1) * inv_freq
    cos, sin = jnp.cos(ang), jnp.sin(ang)
    T = ang.shape[0]
    ones = jnp.ones((T, HEAD_DIM - ROT_DIM), F32)
    zeros = jnp.zeros((T, HEAD_DIM - half), F32)
    c = jnp.concatenate([cos, cos, ones], axis=1)
    sa = jnp.concatenate([-sin, zeros], axis=1)
    sb = jnp.concatenate([jnp.zeros((T, half), F32), sin, zeros[:, half:]], axis=1)
    return c, sa, sb


def _even_mixer(x, g_pre, w_in, w_out, g_post, rope, *, batch, seq):
    scale = HEAD_DIM ** -0.5
    part = N_HEADS_A * HEAD_DIM
    col_scale = jnp.ones((6, 1), F32).at[0].set(scale).at[3].set(scale)
    col_scale = jnp.broadcast_to(col_scale, (6, part)).reshape(1, 6 * part)
    w = (w_in * col_scale).astype(BF16)
    qkv = norm_proj(x, g_pre, w, tm=512, tn=part, out_dtype=BF16, rope=rope, rope_tiles=(0, 1, 3, 4))
    hb = N_HEADS_A
    oa = dilated_attention(qkv, batch=batch, seq=seq, n_heads=N_HEADS_A,
                           q_col=0, k_col=hb, v_col=2 * hb, tq=512)
    kbar = moba_block_means(qkv, batch=batch, seq=seq, n_heads=N_HEADS_B, k_col=4 * hb)
    ob = moba_attention(qkv, kbar, batch=batch, seq=seq, n_heads=N_HEADS_B,
                        q_col=3 * hb, k_col=4 * hb, v_col=5 * hb)
    o = jnp.concatenate([oa, ob], axis=1)
    return proj_norm_res(o, w_out.astype(BF16), x, g_post, tm=512)


def _odd_mixer(x, g_pre, w_in, b_f, w_out, g_post, *, batch, seq):
    scale = HEAD_DIM ** -0.5
    nqkv = 3 * D_MODEL
    col_scale = jnp.concatenate([jnp.full((1, D_MODEL), scale, F32), jnp.ones((1, 2 * D_MODEL), F32)], axis=1)
    w = (w_in[:, :nqkv] * col_scale).astype(BF16)
    qkv = norm_proj(x, g_pre, w, tm=512, tn=1024, out_dtype=BF16)
    wf = jnp.pad(w_in[:, nqkv:], ((0, 0), (0, LANES - N_HEADS))).astype(BF16)
    bf = jnp.pad(b_f.astype(F32), (0, LANES - N_HEADS)).reshape(1, LANES)
    c_col = fox_gate_cumsum(x, g_pre, wf, bf, batch=batch, seq=seq, tm=512)
    tq = 512
    c_row = c_col[:, :N_HEADS].reshape(batch, seq, N_HEADS).transpose(0, 2, 1)
    c_row = c_row.reshape(batch * N_HEADS, seq // tq, tq)
    o = fox_attention(qkv, c_col, c_row, batch=batch, seq=seq, n_heads=N_HEADS, tq=tq)
    return proj_norm_res(o, w_out.astype(BF16), x, g_post, tm=512)


def kernel(x, mem, positions, w_in_ab, w_out_ab, w_in_c, b_f, w_out_c, g_mix_pre, g_mix_post, g_mem_kv, g_mem_pre, g_mem_post, w_mq, w_mk, w_mv, w_mo, g_ffn_pre, g_ffn_post, w_up, conv_w, conv_b, w_down):
    batch, seq, d = x.shape
    mem_tokens = mem.shape[1]
    depth = g_mix_pre.shape[0]
    rope = _rope_tables(positions)
    xs = x.reshape(batch * seq, d)
    mems = mem.reshape(batch * mem_tokens, d)
    mem_scale = MEM_HEAD_DIM ** -0.5
    for layer in range(depth):
        if layer % 2 == 0:
            xs = _even_mixer(xs, g_mix_pre[layer], w_in_ab[layer // 2], w_out_ab[layer // 2],
                             g_mix_post[layer], rope, batch=batch, seq=seq)
        else:
            xs = _odd_mixer(xs, g_mix_pre[layer], w_in_c[layer // 2], b_f[layer // 2], w_out_c[layer // 2],
                            g_mix_post[layer], batch=batch, seq=seq)
        w_kv = jnp.concatenate([w_mk[layer], w_mv[layer]], axis=1).astype(BF16)
        kv = norm_proj(mems, g_mem_kv[layer], w_kv, tm=512, tn=w_kv.shape[1], out_dtype=BF16)
        xs = mem_attention(xs, g_mem_pre[layer], (w_mq[layer] * mem_scale).astype(BF16), kv,
                           w_mo[layer].astype(BF16), g_mem_post[layer],
                           seq=seq, mem_tokens=mem_tokens, tm=512)
        xs = conv_ffn(xs, g_ffn_pre[layer], w_up[layer].astype(BF16), conv_w[layer], conv_b[layer],
                      w_down[layer].astype(BF16), g_ffn_post[layer], seq=seq, tm=512, tf=512)
    return xs.reshape(batch, seq, d)
```

```python
import functools

import numpy as np
import jax
import jax.numpy as jnp
from jax import lax
from jax.experimental import pallas as pl
from jax.experimental.pallas import tpu as pltpu

D_MODEL = 2048
HEAD_DIM = 128
N_HEADS = D_MODEL // HEAD_DIM
N_HEADS_A = N_HEADS // 2
N_HEADS_B = N_HEADS - N_HEADS_A
DILATED_CONFIGS = ((128, 1), (512, 4), (2048, 16))
MOBA_BLOCK = 256
MOBA_TOPK = 3
ROT_DIM = HEAD_DIM // 4
ROPE_THETA = 500000.0
MEM_HEADS = 4
MEM_HEAD_DIM = 128
D_FF = ((8 * D_MODEL // 3 + 255) // 256) * 256
CONV_WIDTH = 3
RMS_EPS = 1e-6

LANES = 128
SUBLANES = 8
NEG = -1e30
LOG2E = 1.4426950408889634
HEADS_PER_STEP = 2
F32 = jnp.float32
BF16 = jnp.bfloat16
MIB = 1024 * 1024


def _params(semantics, vmem_mib):
    return pltpu.CompilerParams(dimension_semantics=semantics,
                                vmem_limit_bytes=vmem_mib * MIB)


def _rms(x, g):
    ms = jnp.mean(x * x, axis=-1, keepdims=True)
    return x * lax.rsqrt(ms + RMS_EPS) * g


def _dot(a, b):
    return jnp.dot(a, b, preferred_element_type=F32)


def _dot_nt(a, b):
    return lax.dot_general(a, b, (((1,), (1,)), ((), ())), preferred_element_type=F32)


def _softmax_init(m_refs, acc_refs):
    for m_ref, acc_ref in zip(m_refs, acc_refs):
        m_ref[...] = jnp.full(m_ref.shape, NEG, F32)
        acc_ref[...] = jnp.zeros(acc_ref.shape, F32)


def _multi_softmax_step(s_list, v_list, m_refs, acc_refs):
    m_prev = [r[...] for r in m_refs]
    m_new = [jnp.maximum(mp, jnp.max(s, axis=1, keepdims=True)) for mp, s in zip(m_prev, s_list)]
    p_list = [jnp.exp2(s - _tile_lanes(mn, s.shape[1])) for s, mn in zip(s_list, m_new)]
    for a, p in enumerate(p_list):
        alpha = jnp.exp2(m_prev[a] - m_new[a])
        acc_ref = acc_refs[a]
        acc_ref[...] = _tile_lanes(alpha, acc_ref.shape[1]) * acc_ref[...] + _dot(p.astype(BF16), v_list[a])
        m_refs[a][...] = m_new[a]


def _softmax_finish(o_ref, acc_refs):
    for a, acc_ref in enumerate(acc_refs):
        acc = acc_ref[...]
        o_ref[:, _head_cols(a)] = (acc[:, :HEAD_DIM] / acc[:, HEAD_DIM:]).astype(o_ref.dtype)


def _tile_lanes(x, width):
    n = width // LANES
    return x if n == 1 else jnp.concatenate([x] * n, axis=1)


def _softmax_scratch(tq, nh):
    return [pltpu.VMEM((tq, LANES), F32)] * nh + [pltpu.VMEM((tq, 2 * HEAD_DIM), F32)] * nh


def _head_cols(a):
    return slice(a * HEAD_DIM, (a + 1) * HEAD_DIM)


def _fill_value_scratch(va_ref, v_ref):
    for a in range(va_ref.shape[0]):
        va_ref[a, :, :HEAD_DIM] = v_ref[:, _head_cols(a)]
        va_ref[a, :, HEAD_DIM:] = jnp.ones((va_ref.shape[1], HEAD_DIM), va_ref.dtype)


def _causal_flash(i, qa_ref, ka_ref, va_ref, m_refs, acc_refs, o_ref):
    nh, tq, _ = qa_ref.shape
    _softmax_init(m_refs, acc_refs)

    def step(first_tile, n_tiles, diagonal):
        start = pl.multiple_of(first_tile * tq, tq)
        width = n_tiles * tq
        s_list = [_dot_nt(qa_ref[a], ka_ref[a, pl.ds(start, width), :]) for a in range(nh)]
        v_list = [va_ref[a, pl.ds(start, width), :] for a in range(nh)]
        if diagonal:
            row = lax.broadcasted_iota(jnp.int32, (tq, width), 0)
            col = lax.broadcasted_iota(jnp.int32, (tq, width), 1)
            s_list = [jnp.where(col - (width - tq) <= row, s, NEG) for s in s_list]
        _multi_softmax_step(s_list, v_list, m_refs, acc_refs)

    def below_diagonal_pair(jj, carry):
        step(2 * jj, 2, False)
        return carry

    lax.fori_loop(0, i // 2, below_diagonal_pair, 0)

    @pl.when(i % 2 == 1)
    def _():
        step(i - 1, 2, True)

    @pl.when(i % 2 == 0)
    def _():
        step(i, 1, True)

    _softmax_finish(o_ref, acc_refs)


def _norm_proj_body(x_ref, g_ref, w_ref, o_ref, h_ref):
    @pl.when(pl.program_id(1) == 0)
    def _():
        h_ref[...] = _rms(x_ref[...], g_ref[...]).astype(BF16)

    o_ref[...] = _dot(h_ref[...], w_ref[...]).astype(o_ref.dtype)


def _norm_proj_rope_body(x_ref, g_ref, w_ref, cos_ref, sin_ref, o_ref, h_ref):
    @pl.when(pl.program_id(1) == 0)
    def _():
        h_ref[...] = _rms(x_ref[...], g_ref[...]).astype(BF16)

    acc = _dot(h_ref[...], w_ref[...])
    c, s = cos_ref[0], sin_ref[0]
    for hh in range(acc.shape[1] // HEAD_DIM):
        t = acc[:, hh * HEAD_DIM:(hh + 1) * HEAD_DIM]
        r = t * c + pltpu.roll(t, HEAD_DIM // 2, 1) * s
        o_ref[:, hh * HEAD_DIM:(hh + 1) * HEAD_DIM] = r.astype(o_ref.dtype)


def norm_proj(x, g, w, *, tm, tn, out_dtype, rope=None, rope_period=None):
    T, D = x.shape
    N = w.shape[1]
    grid = (T // tm, N // tn)
    in_specs = [pl.BlockSpec((tm, D), lambda i, j: (i, 0)),
                pl.BlockSpec((1, D), lambda i, j: (0, 0)),
                pl.BlockSpec((D, tn), lambda i, j: (0, j))]
    args = [x, g.reshape(1, D), w]
    if rope is None:
        body = _norm_proj_body
    else:
        body = _norm_proj_rope_body
        table_spec = pl.BlockSpec((1, tm, HEAD_DIM),
                                  lambda i, j: ((j % rope_period) // (rope_period - 1), i, 0))
        in_specs += [table_spec] * 2
        args += list(rope)
    return pl.pallas_call(
        body,
        grid=grid,
        in_specs=in_specs,
        out_specs=pl.BlockSpec((tm, tn), lambda i, j: (i, j)),
        out_shape=jax.ShapeDtypeStruct((T, N), out_dtype),
        scratch_shapes=[pltpu.VMEM((tm, D), BF16)],
        compiler_params=_params(("parallel", "arbitrary"), 48),
        name="norm_proj",
    )(*args)


def _proj_norm_res_body(*refs):
    *a_refs, w_ref, x_ref, g_ref, o_ref = refs
    y, row = None, 0
    for a_ref in a_refs:
        k = a_ref.shape[1]
        part = _dot(a_ref[...], w_ref[row:row + k, :])
        y = part if y is None else y + part
        row += k
    o_ref[...] = x_ref[...] + _rms(y, g_ref[...])


def proj_norm_res(a_parts, w, x, g, *, tm):
    T = x.shape[0]
    K, D = w.shape
    return pl.pallas_call(
        _proj_norm_res_body,
        grid=(T // tm,),
        in_specs=[pl.BlockSpec((tm, a.shape[1]), lambda i: (i, 0)) for a in a_parts]
                 + [pl.BlockSpec((K, D), lambda i: (0, 0)),
                    pl.BlockSpec((tm, D), lambda i: (i, 0)),
                    pl.BlockSpec((1, D), lambda i: (0, 0))],
        out_specs=pl.BlockSpec((tm, D), lambda i: (i, 0)),
        out_shape=jax.ShapeDtypeStruct((T, D), F32),
        compiler_params=_params(("parallel",), 48),
        name="proj_norm_res",
    )(*a_parts, w, x, g.reshape(1, D))


def _dilated_log_multiplicity(tq, n_back):
    r = np.arange(tq)[:, None]
    c = np.arange(tq)[None, :]
    out = []
    for j in range(n_back + 1):
        d = (n_back - j) * tq + r - c
        mult = np.zeros((tq, tq), np.float64)
        for window, dil in DILATED_CONFIGS:
            mult += (d >= 0) & (d % dil == 0) & (d <= window)
        out.append(np.where(mult > 0, np.log2(np.maximum(mult, 1.0)), NEG))
    return np.stack(out).astype(np.float32)


def _dilated_body(q_ref, k_ref, v_ref, lb_ref, o_ref, va_ref, *scratch, n_back):
    nh = HEADS_PER_STEP
    m_refs, acc_refs = scratch[:nh], scratch[nh:]
    i = pl.program_id(2)
    tq = q_ref.shape[0]

    @pl.when(i == 0)
    def _():
        _fill_value_scratch(va_ref, v_ref)

    _softmax_init(m_refs, acc_refs)

    def kv_tile(j, carry):
        start = pl.multiple_of((i - n_back + j) * tq, tq)
        lb = lb_ref[j]
        s_list = [_dot_nt(q_ref[:, _head_cols(a)], k_ref[pl.ds(start, tq), _head_cols(a)]) + lb
                  for a in range(nh)]
        v_list = [va_ref[a, pl.ds(start, tq), :] for a in range(nh)]
        _multi_softmax_step(s_list, v_list, m_refs, acc_refs)
        return carry

    lax.fori_loop(jnp.maximum(n_back - i, 0), n_back + 1, kv_tile, 0)
    _softmax_finish(o_ref, acc_refs)


def dilated_attention(qkv, *, batch, seq, n_heads, q_col, k_col, v_col, tq):
    nq = seq // tq
    nh = HEADS_PER_STEP
    groups = n_heads // nh
    width = nh * HEAD_DIM
    max_window = max(w for w, _ in DILATED_CONFIGS)
    n_back = -(-max_window // tq)
    lb = jnp.asarray(_dilated_log_multiplicity(tq, n_back))
    qg, kg, vg = q_col // nh, k_col // nh, v_col // nh
    return pl.pallas_call(
        functools.partial(_dilated_body, n_back=n_back),
        grid=(batch, groups, nq),
        in_specs=[pl.BlockSpec((tq, width), lambda b, g, i: (b * nq + i, qg + g)),
                  pl.BlockSpec((seq, width), lambda b, g, i: (b, kg + g)),
                  pl.BlockSpec((seq, width), lambda b, g, i: (b, vg + g)),
                  pl.BlockSpec((n_back + 1, tq, tq), lambda b, g, i: (0, 0, 0))],
        out_specs=pl.BlockSpec((tq, width), lambda b, g, i: (b * nq + i, g)),
        out_shape=jax.ShapeDtypeStruct((batch * seq, n_heads * HEAD_DIM), BF16),
        scratch_shapes=[pltpu.VMEM((nh, seq, 2 * HEAD_DIM), BF16)] + _softmax_scratch(tq, nh),
        compiler_params=_params(("parallel", "parallel", "arbitrary"), 40),
        name="dilated_attention",
    )(qkv, qkv, qkv, lb)


def _block_mean_body(k_ref, o_ref):
    n = pl.program_id(1)
    mean = jnp.mean(k_ref[...].astype(F32), axis=0, keepdims=True)
    o_ref[0, pl.ds(n, 1), :] = mean


def moba_block_means(qkv, *, batch, seq, n_heads, k_col):
    nblk = seq // MOBA_BLOCK
    width = n_heads * HEAD_DIM
    kc = k_col * HEAD_DIM // width
    return pl.pallas_call(
        _block_mean_body,
        grid=(batch, nblk),
        in_specs=[pl.BlockSpec((MOBA_BLOCK, width), lambda b, n: (b * nblk + n, kc))],
        out_specs=pl.BlockSpec((1, nblk, width), lambda b, n: (b, 0, 0)),
        out_shape=jax.ShapeDtypeStruct((batch, nblk, width), F32),
        compiler_params=_params(("parallel", "arbitrary"), 16),
        name="moba_block_means",
    )(qkv)


def _moba_select_bias(q, kbar, own, nblk):
    kbar = jnp.concatenate([kbar, jnp.zeros((LANES - nblk, HEAD_DIM), F32)], axis=0)
    kb_hi = kbar.astype(BF16)
    kb_lo = (kbar - kb_hi.astype(F32)).astype(BF16)
    gate = _dot_nt(q, kb_hi) + _dot_nt(q, kb_lo)
    lane = lax.broadcasted_iota(jnp.int32, gate.shape, 1)
    lane_f = lane.astype(F32)
    past = lane < own
    g = jnp.where(past, gate, -jnp.inf)
    selected = jnp.zeros(gate.shape, jnp.bool_)
    for _ in range(MOBA_TOPK):
        mx = jnp.max(g, axis=1, keepdims=True)
        first = jnp.min(jnp.where(g == mx, lane_f, float(LANES)), axis=1, keepdims=True)
        pick = (lane_f == first) & past
        selected = selected | pick
        g = jnp.where(pick, -jnp.inf, g)
    return jnp.where(selected | (lane == own), 0.0, NEG)


def _moba_body(q_ref, k_ref, v_ref, kbar_ref, onehot_ref, o_ref, qa_ref, ka_ref, va_ref, *scratch, nblk):
    nh = HEADS_PER_STEP
    m_refs, acc_refs = scratch[:nh], scratch[nh:]
    i = pl.program_id(2)
    tq = q_ref.shape[0]

    @pl.when(i == 0)
    def _():
        _fill_value_scratch(va_ref, v_ref)
        for a in range(nh):
            ka_ref[a, :, :HEAD_DIM] = k_ref[:, _head_cols(a)]
            ka_ref[a, :, HEAD_DIM:] = onehot_ref[...]

    row = lax.broadcasted_iota(jnp.int32, (tq, LANES), 0)
    own = (i * tq + row) // MOBA_BLOCK
    for a in range(nh):
        q = q_ref[:, _head_cols(a)]
        qa_ref[a, :, :HEAD_DIM] = q
        qa_ref[a, :, HEAD_DIM:] = _moba_select_bias(q, kbar_ref[0, :, _head_cols(a)], own, nblk).astype(BF16)

    _causal_flash(i, qa_ref, ka_ref, va_ref, m_refs, acc_refs, o_ref)


def moba_attention(qkv, kbar, *, batch, seq, n_heads, q_col, k_col, v_col, tq):
    nblk = seq // MOBA_BLOCK
    nq = seq // tq
    nh = HEADS_PER_STEP
    groups = n_heads // nh
    width = nh * HEAD_DIM
    qg, kg, vg = q_col // nh, k_col // nh, v_col // nh
    onehot = (np.arange(seq)[:, None] // MOBA_BLOCK == np.arange(LANES)[None, :]).astype(np.float32)
    return pl.pallas_call(
        functools.partial(_moba_body, nblk=nblk),
        grid=(batch, groups, nq),
        in_specs=[pl.BlockSpec((tq, width), lambda b, g, i: (b * nq + i, qg + g)),
                  pl.BlockSpec((seq, width), lambda b, g, i: (b, kg + g)),
                  pl.BlockSpec((seq, width), lambda b, g, i: (b, vg + g)),
                  pl.BlockSpec((1, nblk, width), lambda b, g, i: (b, 0, g)),
                  pl.BlockSpec((seq, LANES), lambda b, g, i: (0, 0))],
        out_specs=pl.BlockSpec((tq, width), lambda b, g, i: (b * nq + i, g)),
        out_shape=jax.ShapeDtypeStruct((batch * seq, n_heads * HEAD_DIM), BF16),
        scratch_shapes=[pltpu.VMEM((nh, tq, 2 * HEAD_DIM), BF16),
                        pltpu.VMEM((nh, seq, 2 * HEAD_DIM), BF16),
                        pltpu.VMEM((nh, seq, 2 * HEAD_DIM), BF16)] + _softmax_scratch(tq, nh),
        compiler_params=_params(("parallel", "parallel", "arbitrary"), 40),
        name="moba_attention",
    )(qkv, qkv, qkv, kbar, jnp.asarray(onehot, BF16))


def _fox_gate_body(x_ref, g_ref, wf_ref, bf_ref, tri_ref, c_ref, carry_ref):
    @pl.when(pl.program_id(1) == 0)
    def _():
        carry_ref[...] = jnp.zeros(carry_ref.shape, F32)

    h = _rms(x_ref[...], g_ref[...]).astype(BF16)
    fg = _dot(h, wf_ref[...]) + bf_ref[...]
    logf = (jnp.minimum(fg, 0.0) - jnp.log(1.0 + jnp.exp(-jnp.abs(fg)))) * LOG2E
    hi = logf.astype(BF16)
    rem = logf - hi.astype(F32)
    mid = rem.astype(BF16)
    lo = (rem - mid.astype(F32)).astype(BF16)
    tri = tri_ref[...]
    c = _dot(tri, hi) + _dot(tri, mid) + _dot(tri, lo) + carry_ref[0:1, :]
    c_ref[...] = c
    carry_ref[...] = jnp.broadcast_to(c[c.shape[0] - 1:, :], carry_ref.shape)


def fox_gate_cumsum(x, g, wf, bf, *, batch, seq, tm):
    T, D = x.shape
    ns = seq // tm
    tri = jnp.asarray(np.tril(np.ones((tm, tm), np.float32)), BF16)
    return pl.pallas_call(
        _fox_gate_body,
        grid=(batch, ns),
        in_specs=[pl.BlockSpec((tm, D), lambda b, s: (b * ns + s, 0)),
                  pl.BlockSpec((1, D), lambda b, s: (0, 0)),
                  pl.BlockSpec((D, LANES), lambda b, s: (0, 0)),
                  pl.BlockSpec((1, LANES), lambda b, s: (0, 0)),
                  pl.BlockSpec((tm, tm), lambda b, s: (0, 0))],
        out_specs=pl.BlockSpec((tm, LANES), lambda b, s: (b * ns + s, 0)),
        out_shape=jax.ShapeDtypeStruct((T, LANES), F32),
        scratch_shapes=[pltpu.VMEM((SUBLANES, LANES), F32)],
        compiler_params=_params(("parallel", "arbitrary"), 32),
        name="fox_gate_cumsum",
    )(x, g.reshape(1, D), wf, bf, tri)


def _split3(c):
    hi = c.astype(BF16).astype(F32)
    mid = (c - hi).astype(BF16).astype(F32)
    lo = (c - hi - mid).astype(BF16).astype(F32)
    return hi, mid, lo


def _fox_bias_factors(cc, head, negate):
    lane = lax.broadcasted_iota(jnp.int32, cc.shape, 1)
    c = jnp.sum(jnp.where(lane == head, cc, 0.0), axis=1, keepdims=True)
    hi, mid, lo = _split3(-c if negate else c)
    base = 3 if negate else 0
    terms = jnp.where(lane == base, hi, jnp.where(lane == base + 1, mid, lo))
    is_term = (lane >= base) & (lane < base + 3)
    return jnp.where(is_term, terms, jnp.where(lane < 6, 1.0, 0.0))


def _fox_body(q_ref, k_ref, v_ref, c_ref, o_ref, qa_ref, ka_ref, va_ref, *scratch):
    nh = HEADS_PER_STEP
    m_refs, acc_refs = scratch[:nh], scratch[nh:]
    g = pl.program_id(1)
    i = pl.program_id(2)
    tq = q_ref.shape[0]

    @pl.when(i == 0)
    def _():
        _fill_value_scratch(va_ref, v_ref)
        cc = c_ref[...]
        for a in range(nh):
            ka_ref[a, :, :HEAD_DIM] = k_ref[:, _head_cols(a)]
            ka_ref[a, :, HEAD_DIM:] = _fox_bias_factors(cc, g * nh + a, True).astype(BF16)

    cc = c_ref[pl.ds(pl.multiple_of(i * tq, tq), tq), :]
    for a in range(nh):
        qa_ref[a, :, :HEAD_DIM] = q_ref[:, _head_cols(a)]
        qa_ref[a, :, HEAD_DIM:] = _fox_bias_factors(cc, g * nh + a, False).astype(BF16)

    _causal_flash(i, qa_ref, ka_ref, va_ref, m_refs, acc_refs, o_ref)


def fox_attention(qkv, c_col, *, batch, seq, n_heads, tq):
    nq = seq // tq
    nh = HEADS_PER_STEP
    groups = n_heads // nh
    width = nh * HEAD_DIM
    return pl.pallas_call(
        _fox_body,
        grid=(batch, groups, nq),
        in_specs=[pl.BlockSpec((tq, width), lambda b, g, i: (b * nq + i, g)),
                  pl.BlockSpec((seq, width), lambda b, g, i: (b, groups + g)),
                  pl.BlockSpec((seq, width), lambda b, g, i: (b, 2 * groups + g)),
                  pl.BlockSpec((seq, LANES), lambda b, g, i: (b, 0))],
        out_specs=pl.BlockSpec((tq, width), lambda b, g, i: (b * nq + i, g)),
        out_shape=jax.ShapeDtypeStruct((batch * seq, n_heads * HEAD_DIM), BF16),
        scratch_shapes=[pltpu.VMEM((nh, tq, 2 * HEAD_DIM), BF16),
                        pltpu.VMEM((nh, seq, 2 * HEAD_DIM), BF16),
                        pltpu.VMEM((nh, seq, 2 * HEAD_DIM), BF16)] + _softmax_scratch(tq, nh),
        compiler_params=_params(("parallel", "parallel", "arbitrary"), 40),
        name="fox_attention",
    )(qkv, qkv, qkv, c_col)


def _mem_attn_body(x_ref, gpre_ref, wq_ref, kv_ref, wo_ref, gpost_ref, o_ref):
    x = x_ref[...]
    h = _rms(x, gpre_ref[...]).astype(BF16)
    q = _dot(h, wq_ref[...]).astype(BF16)
    width = MEM_HEADS * MEM_HEAD_DIM
    outs = []
    for hd in range(MEM_HEADS):
        lo, hi = hd * MEM_HEAD_DIM, (hd + 1) * MEM_HEAD_DIM
        s = _dot_nt(q[:, lo:hi], kv_ref[:, lo:hi])
        p = jnp.exp(s - jnp.max(s, axis=1, keepdims=True))
        o = _dot(p.astype(BF16), kv_ref[:, width + lo:width + hi])
        outs.append((o / jnp.sum(p, axis=1, keepdims=True)).astype(BF16))
    y = _dot(jnp.concatenate(outs, axis=1), wo_ref[...])
    o_ref[...] = x + _rms(y, gpost_ref[...])


def mem_attention(x, gpre, wq, kv, wo, gpost, *, seq, mem_tokens, tm):
    T, D = x.shape
    width = wq.shape[1]
    per_seq = seq // tm
    return pl.pallas_call(
        _mem_attn_body,
        grid=(T // tm,),
        in_specs=[pl.BlockSpec((tm, D), lambda i: (i, 0)),
                  pl.BlockSpec((1, D), lambda i: (0, 0)),
                  pl.BlockSpec((D, width), lambda i: (0, 0)),
                  pl.BlockSpec((mem_tokens, 2 * width), lambda i: (i // per_seq, 0)),
                  pl.BlockSpec((width, D), lambda i: (0, 0)),
                  pl.BlockSpec((1, D), lambda i: (0, 0))],
        out_specs=pl.BlockSpec((tm, D), lambda i: (i, 0)),
        out_shape=jax.ShapeDtypeStruct((T, D), F32),
        compiler_params=_params(("parallel",), 48),
        name="mem_attention",
    )(x, gpre.reshape(1, D), wq, kv, wo, gpost.reshape(1, D))


def _ffn_body(x_ref, gpre_ref, wg_ref, wv_ref, cwg_ref, cwv_ref, cbg_ref, cbv_ref, wd_ref, gpost_ref,
              o_ref, h_ref, acc_ref, ubuf_ref, carry_ref, *, tiles_per_seq):
    i = pl.program_id(0)
    f = pl.program_id(1)
    tm = x_ref.shape[0]
    halo = SUBLANES

    @pl.when(f == 0)
    def _():
        h_ref[...] = _rms(x_ref[...], gpre_ref[...]).astype(BF16)
        acc_ref[...] = jnp.zeros(acc_ref.shape, F32)

    h = h_ref[...]
    seq_start = i % tiles_per_seq == 0

    def conv_branch(w_ref, cw_ref, cb_ref, slot):
        u = _dot(h, w_ref[...])
        ubuf_ref[slot, 0:halo, :] = jnp.where(seq_start, 0.0, carry_ref[f, slot])
        ubuf_ref[slot, halo:, :] = u
        carry_ref[f, slot] = u[tm - halo:, :]
        cw = cw_ref[...]
        return (cb_ref[...] + cw[0:1] * ubuf_ref[slot, halo - 2:halo - 2 + tm, :]
                + cw[1:2] * ubuf_ref[slot, halo - 1:halo - 1 + tm, :] + cw[2:3] * u)

    g = conv_branch(wg_ref, cwg_ref, cbg_ref, 0)
    v = conv_branch(wv_ref, cwv_ref, cbv_ref, 1)
    a = (g / (1.0 + jnp.exp(-g)) * v).astype(BF16)
    acc_ref[...] += _dot(a, wd_ref[...])

    @pl.when(f == pl.num_programs(1) - 1)
    def _():
        o_ref[...] = x_ref[...] + _rms(acc_ref[...], gpost_ref[...])


def conv_ffn(x, gpre, w_up, conv_w, conv_b, w_down, gpost, *, seq, tm, tf):
    T, D = x.shape
    dff = w_down.shape[0]
    nf = dff // tf
    cb = conv_b.reshape(1, 2 * dff)
    return pl.pallas_call(
        functools.partial(_ffn_body, tiles_per_seq=seq // tm),
        grid=(T // tm, nf),
        in_specs=[pl.BlockSpec((tm, D), lambda i, f: (i, 0)),
                  pl.BlockSpec((1, D), lambda i, f: (0, 0)),
                  pl.BlockSpec((D, tf), lambda i, f: (0, f)),
                  pl.BlockSpec((D, tf), lambda i, f: (0, f + nf)),
                  pl.BlockSpec((CONV_WIDTH, tf), lambda i, f: (0, f)),
                  pl.BlockSpec((CONV_WIDTH, tf), lambda i, f: (0, f + nf)),
                  pl.BlockSpec((1, tf), lambda i, f: (0, f)),
                  pl.BlockSpec((1, tf), lambda i, f: (0, f + nf)),
                  pl.BlockSpec((tf, D), lambda i, f: (f, 0)),
                  pl.BlockSpec((1, D), lambda i, f: (0, 0))],
        out_specs=pl.BlockSpec((tm, D), lambda i, f: (i, 0)),
        out_shape=jax.ShapeDtypeStruct((T, D), F32),
        scratch_shapes=[pltpu.VMEM((tm, D), BF16),
                        pltpu.VMEM((tm, D), F32),
                        pltpu.VMEM((2, SUBLANES + tm, tf), F32),
                        pltpu.VMEM((nf, 2, SUBLANES, tf), F32)],
        compiler_params=_params(("arbitrary", "arbitrary"), 56),
        name="conv_ffn",
    )(x, gpre.reshape(1, D), w_up, w_up, conv_w, conv_w, cb, cb, w_down, gpost.reshape(1, D))


def _rope_head_permutation():
    half = ROT_DIM // 2
    mid = HEAD_DIM // 2
    return np.concatenate([np.arange(0, half), np.arange(ROT_DIM, mid + half),
                           np.arange(half, ROT_DIM), np.arange(mid + half, HEAD_DIM)])


def _rope_tables(positions):
    half = ROT_DIM // 2
    mid = HEAD_DIM // 2
    inv_freq = ROPE_THETA ** (-jnp.arange(0, ROT_DIM, 2, dtype=F32) / ROT_DIM)
    ang = positions.astype(F32).reshape(-1, 1) * inv_freq
    cos, sin = jnp.cos(ang), jnp.sin(ang)
    T = ang.shape[0]
    ones = jnp.ones((T, mid - half), F32)
    zeros = jnp.zeros((T, mid - half), F32)
    c = jnp.concatenate([cos, ones, cos, ones], axis=1)
    s = jnp.concatenate([-sin, zeros, sin, zeros], axis=1)
    return jnp.stack([c, jnp.ones_like(c)]), jnp.stack([s, jnp.zeros_like(s)])


def _even_mixer(x, g_pre, w_in, w_out, g_post, rope, *, batch, seq):
    scale = HEAD_DIM ** -0.5 * LOG2E
    part = N_HEADS_A * HEAD_DIM
    col_scale = jnp.ones((6, 1), F32).at[0].set(scale).at[3].set(scale)
    col_scale = jnp.broadcast_to(col_scale, (6, part)).reshape(1, 6 * part)
    cols = np.arange(6 * part).reshape(6, N_HEADS_A, HEAD_DIM)
    cols[[0, 1, 3, 4]] = cols[[0, 1, 3, 4]][:, :, _rope_head_permutation()]
    w = (w_in * col_scale)[:, cols.reshape(-1)].astype(BF16)
    qkv = norm_proj(x, g_pre, w, tm=512, tn=part, out_dtype=BF16, rope=rope, rope_period=3)
    hb = N_HEADS_A
    oa = dilated_attention(qkv, batch=batch, seq=seq, n_heads=N_HEADS_A,
                           q_col=0, k_col=hb, v_col=2 * hb, tq=512)
    kbar = moba_block_means(qkv, batch=batch, seq=seq, n_heads=N_HEADS_B, k_col=4 * hb)
    ob = moba_attention(qkv, kbar, batch=batch, seq=seq, n_heads=N_HEADS_B,
                        q_col=3 * hb, k_col=4 * hb, v_col=5 * hb, tq=512)
    return proj_norm_res([oa, ob], w_out.astype(BF16), x, g_post, tm=512)


def _odd_mixer(x, g_pre, w_in, b_f, w_out, g_post, *, batch, seq):
    scale = HEAD_DIM ** -0.5 * LOG2E
    nqkv = 3 * D_MODEL
    col_scale = jnp.concatenate([jnp.full((1, D_MODEL), scale, F32), jnp.ones((1, 2 * D_MODEL), F32)], axis=1)
    w = (w_in[:, :nqkv] * col_scale).astype(BF16)
    qkv = norm_proj(x, g_pre, w, tm=512, tn=1024, out_dtype=BF16)
    wf = jnp.pad(w_in[:, nqkv:], ((0, 0), (0, LANES - N_HEADS))).astype(BF16)
    bf = jnp.pad(b_f.astype(F32), (0, LANES - N_HEADS)).reshape(1, LANES)
    c_col = fox_gate_cumsum(x, g_pre, wf, bf, batch=batch, seq=seq, tm=512)
    o = fox_attention(qkv, c_col, batch=batch, seq=seq, n_heads=N_HEADS, tq=512)
    return proj_norm_res([o], w_out.astype(BF16), x, g_post, tm=512)


def kernel(x, mem, positions, w_in_ab, w_out_ab, w_in_c, b_f, w_out_c, g_mix_pre, g_mix_post, g_mem_kv, g_mem_pre, g_mem_post, w_mq, w_mk, w_mv, w_mo, g_ffn_pre, g_ffn_post, w_up, conv_w, conv_b, w_down):
    batch, seq, d = x.shape
    mem_tokens = mem.shape[1]
    depth = g_mix_pre.shape[0]
    rope = _rope_tables(positions)
    xs = x.reshape(batch * seq, d)
    mems = mem.reshape(batch * mem_tokens, d)
    mem_scale = MEM_HEAD_DIM ** -0.5
    for layer in range(depth):
        if layer % 2 == 0:
            xs = _even_mixer(xs, g_mix_pre[layer], w_in_ab[layer // 2], w_out_ab[layer // 2],
                             g_mix_post[layer], rope, batch=batch, seq=seq)
        else:
            xs = _odd_mixer(xs, g_mix_pre[layer], w_in_c[layer // 2], b_f[layer // 2], w_out_c[layer // 2],
                            g_mix_post[layer], batch=batch, seq=seq)
        w_kv = jnp.concatenate([w_mk[layer], w_mv[layer]], axis=1).astype(BF16)
        kv = norm_proj(mems, g_mem_kv[layer], w_kv, tm=512, tn=w_kv.shape[1], out_dtype=BF16)
        xs = mem_attention(xs, g_mem_pre[layer], (w_mq[layer] * mem_scale).astype(BF16), kv,
                           w_mo[layer].astype(BF16), g_mem_post[layer],
                           seq=seq, mem_tokens=mem_tokens, tm=512)
        xs = conv_ffn(xs, g_ffn_pre[layer], w_up[layer].astype(BF16), conv_w[layer], conv_b[layer],
                      w_down[layer].astype(BF16), g_ffn_post[layer], seq=seq, tm=512, tf=512)
    return xs.reshape(batch, seq, d)
```

```python
import functools

import numpy as np
import jax
import jax.numpy as jnp
from jax import lax
from jax.experimental import pallas as pl
from jax.experimental.pallas import tpu as pltpu

D_MODEL = 2048
HEAD_DIM = 128
N_HEADS = D_MODEL // HEAD_DIM
N_HEADS_A = N_HEADS // 2
N_HEADS_B = N_HEADS - N_HEADS_A
DILATED_CONFIGS = ((128, 1), (512, 4), (2048, 16))
MOBA_BLOCK = 256
MOBA_TOPK = 3
ROT_DIM = HEAD_DIM // 4
ROPE_THETA = 500000.0
MEM_HEADS = 4
MEM_HEAD_DIM = 128
D_FF = ((8 * D_MODEL // 3 + 255) // 256) * 256
CONV_WIDTH = 3
RMS_EPS = 1e-6

LANES = 128
SUBLANES = 8
NEG = -1e30
LOG2E = 1.4426950408889634
HEADS_PER_STEP = 2
F32 = jnp.float32
BF16 = jnp.bfloat16
MIB = 1024 * 1024


def _params(semantics, vmem_mib):
    return pltpu.CompilerParams(dimension_semantics=semantics,
                                vmem_limit_bytes=vmem_mib * MIB)


def _rms(x, g):
    ms = jnp.mean(x * x, axis=-1, keepdims=True)
    return x * lax.rsqrt(ms + RMS_EPS) * g


def _dot(a, b):
    return jnp.dot(a, b, preferred_element_type=F32)


def _dot_nt(a, b):
    return lax.dot_general(a, b, (((1,), (1,)), ((), ())), preferred_element_type=F32)


def _softmax_init(m_refs, acc_refs):
    for m_ref, acc_ref in zip(m_refs, acc_refs):
        m_ref[...] = jnp.full(m_ref.shape, NEG, F32)
        acc_ref[...] = jnp.zeros(acc_ref.shape, F32)


def _multi_softmax_step(s_list, v_list, m_refs, acc_refs):
    m_prev = [r[...] for r in m_refs]
    m_new = [jnp.maximum(mp, jnp.max(s, axis=1, keepdims=True)) for mp, s in zip(m_prev, s_list)]
    p_list = [jnp.exp2(s - _tile_lanes(mn, s.shape[1])) for s, mn in zip(s_list, m_new)]
    for a, p in enumerate(p_list):
        alpha = jnp.exp2(m_prev[a] - m_new[a])
        acc_ref = acc_refs[a]
        acc_ref[...] = _tile_lanes(alpha, acc_ref.shape[1]) * acc_ref[...] + _dot(p.astype(BF16), v_list[a])
        m_refs[a][...] = m_new[a]


def _softmax_finish(o_ref, acc_refs):
    for a, acc_ref in enumerate(acc_refs):
        acc = acc_ref[...]
        o_ref[:, _head_cols(a)] = (acc[:, :HEAD_DIM] / acc[:, HEAD_DIM:]).astype(o_ref.dtype)


def _tile_lanes(x, width):
    n = width // LANES
    return x if n == 1 else jnp.concatenate([x] * n, axis=1)


def _softmax_scratch(tq, nh):
    return [pltpu.VMEM((tq, LANES), F32)] * nh + [pltpu.VMEM((tq, 2 * HEAD_DIM), F32)] * nh


def _head_cols(a):
    return slice(a * HEAD_DIM, (a + 1) * HEAD_DIM)


def _fill_value_scratch(va_ref, v_ref):
    for a in range(va_ref.shape[0]):
        va_ref[a, :, :HEAD_DIM] = v_ref[:, _head_cols(a)]
        va_ref[a, :, HEAD_DIM:] = jnp.ones((va_ref.shape[1], HEAD_DIM), va_ref.dtype)


def _causal_flash(i, qa_ref, ka_ref, va_ref, m_refs, acc_refs, o_ref):
    nh, tq, _ = qa_ref.shape
    _softmax_init(m_refs, acc_refs)

    def step(first_tile, n_tiles, diagonal):
        start = pl.multiple_of(first_tile * tq, tq)
        width = n_tiles * tq
        s_list = [_dot_nt(qa_ref[a], ka_ref[a, pl.ds(start, width), :]) for a in range(nh)]
        v_list = [va_ref[a, pl.ds(start, width), :] for a in range(nh)]
        if diagonal:
            row = lax.broadcasted_iota(jnp.int32, (tq, width), 0)
            col = lax.broadcasted_iota(jnp.int32, (tq, width), 1)
            s_list = [jnp.where(col - (width - tq) <= row, s, NEG) for s in s_list]
        _multi_softmax_step(s_list, v_list, m_refs, acc_refs)

    def below_diagonal_pair(jj, carry):
        step(2 * jj, 2, False)
        return carry

    lax.fori_loop(0, i // 2, below_diagonal_pair, 0)

    @pl.when(i % 2 == 1)
    def _():
        step(i - 1, 2, True)

    @pl.when(i % 2 == 0)
    def _():
        step(i, 1, True)

    _softmax_finish(o_ref, acc_refs)


def _norm_proj_body(x_ref, g_ref, w_ref, o_ref, h_ref):
    @pl.when(pl.program_id(1) == 0)
    def _():
        h_ref[...] = _rms(x_ref[...], g_ref[...]).astype(BF16)

    o_ref[...] = _dot(h_ref[...], w_ref[...]).astype(o_ref.dtype)


def _norm_proj_rope_body(x_ref, g_ref, w_ref, cos_ref, sin_ref, o_ref, h_ref):
    @pl.when(pl.program_id(1) == 0)
    def _():
        h_ref[...] = _rms(x_ref[...], g_ref[...]).astype(BF16)

    acc = _dot(h_ref[...], w_ref[...])
    c, s = cos_ref[0], sin_ref[0]
    for hh in range(acc.shape[1] // HEAD_DIM):
        t = acc[:, hh * HEAD_DIM:(hh + 1) * HEAD_DIM]
        r = t * c + pltpu.roll(t, HEAD_DIM // 2, 1) * s
        o_ref[:, hh * HEAD_DIM:(hh + 1) * HEAD_DIM] = r.astype(o_ref.dtype)


def norm_proj(x, g, w, *, tm, tn, out_dtype, rope=None, rope_period=None):
    T, D = x.shape
    N = w.shape[1]
    grid = (T // tm, N // tn)
    in_specs = [pl.BlockSpec((tm, D), lambda i, j: (i, 0)),
                pl.BlockSpec((1, D), lambda i, j: (0, 0)),
                pl.BlockSpec((D, tn), lambda i, j: (0, j))]
    args = [x, g.reshape(1, D), w]
    if rope is None:
        body = _norm_proj_body
    else:
        body = _norm_proj_rope_body
        table_spec = pl.BlockSpec((1, tm, HEAD_DIM),
                                  lambda i, j: ((j % rope_period) // (rope_period - 1), i, 0))
        in_specs += [table_spec] * 2
        args += list(rope)
    return pl.pallas_call(
        body,
        grid=grid,
        in_specs=in_specs,
        out_specs=pl.BlockSpec((tm, tn), lambda i, j: (i, j)),
        out_shape=jax.ShapeDtypeStruct((T, N), out_dtype),
        scratch_shapes=[pltpu.VMEM((tm, D), BF16)],
        compiler_params=_params(("parallel", "arbitrary"), 48),
        name="norm_proj",
    )(*args)


def _proj_norm_res_body(*refs):
    *a_refs, w_ref, x_ref, g_ref, o_ref = refs
    y, row = None, 0
    for a_ref in a_refs:
        k = a_ref.shape[1]
        part = _dot(a_ref[...], w_ref[row:row + k, :])
        y = part if y is None else y + part
        row += k
    o_ref[...] = x_ref[...] + _rms(y, g_ref[...])


def proj_norm_res(a_parts, w, x, g, *, tm):
    T = x.shape[0]
    K, D = w.shape
    return pl.pallas_call(
        _proj_norm_res_body,
        grid=(T // tm,),
        in_specs=[pl.BlockSpec((tm, a.shape[1]), lambda i: (i, 0)) for a in a_parts]
                 + [pl.BlockSpec((K, D), lambda i: (0, 0)),
                    pl.BlockSpec((tm, D), lambda i: (i, 0)),
                    pl.BlockSpec((1, D), lambda i: (0, 0))],
        out_specs=pl.BlockSpec((tm, D), lambda i: (i, 0)),
        out_shape=jax.ShapeDtypeStruct((T, D), F32),
        compiler_params=_params(("parallel",), 48),
        name="proj_norm_res",
    )(*a_parts, w, x, g.reshape(1, D))


def _dilated_log_multiplicity(tq, n_back):
    r = np.arange(tq)[:, None]
    c = np.arange(tq)[None, :]
    out = []
    for j in range(n_back + 1):
        d = (n_back - j) * tq + r - c
        mult = np.zeros((tq, tq), np.float64)
        for window, dil in DILATED_CONFIGS:
            mult += (d >= 0) & (d % dil == 0) & (d <= window)
        out.append(np.where(mult > 0, np.log2(np.maximum(mult, 1.0)), NEG))
    return np.stack(out).astype(np.float32)


def _dilated_body(q_ref, k_ref, v_ref, lb_ref, o_ref, va_ref, *scratch, n_back):
    nh = HEADS_PER_STEP
    m_refs, acc_refs = scratch[:nh], scratch[nh:]
    i = pl.program_id(2)
    tq = q_ref.shape[0]

    @pl.when(i == 0)
    def _():
        _fill_value_scratch(va_ref, v_ref)

    _softmax_init(m_refs, acc_refs)

    def step(first_j, n_tiles):
        start = pl.multiple_of((i - n_back + first_j) * tq, tq)
        width = n_tiles * tq
        lb = [lb_ref[first_j + t] for t in range(n_tiles)]
        lb = lb[0] if n_tiles == 1 else jnp.concatenate(lb, axis=1)
        s_list = [_dot_nt(q_ref[:, _head_cols(a)], k_ref[pl.ds(start, width), _head_cols(a)]) + lb
                  for a in range(nh)]
        v_list = [va_ref[a, pl.ds(start, width), :] for a in range(nh)]
        _multi_softmax_step(s_list, v_list, m_refs, acc_refs)

    total = jnp.minimum(i, n_back) + 1
    j_lo = n_back + 1 - total

    @pl.when(total % 2 == 1)
    def _():
        step(j_lo, 1)

    def pair(p, carry):
        step(j_lo + total % 2 + 2 * p, 2)
        return carry

    lax.fori_loop(0, total // 2, pair, 0)
    _softmax_finish(o_ref, acc_refs)


def dilated_attention(qkv, *, batch, seq, n_heads, q_col, k_col, v_col, tq):
    nq = seq // tq
    nh = HEADS_PER_STEP
    groups = n_heads // nh
    width = nh * HEAD_DIM
    max_window = max(w for w, _ in DILATED_CONFIGS)
    n_back = -(-max_window // tq)
    lb = jnp.asarray(_dilated_log_multiplicity(tq, n_back))
    qg, kg, vg = q_col // nh, k_col // nh, v_col // nh
    return pl.pallas_call(
        functools.partial(_dilated_body, n_back=n_back),
        grid=(batch, groups, nq),
        in_specs=[pl.BlockSpec((tq, width), lambda b, g, i: (b * nq + i, qg + g)),
                  pl.BlockSpec((seq, width), lambda b, g, i: (b, kg + g)),
                  pl.BlockSpec((seq, width), lambda b, g, i: (b, vg + g)),
                  pl.BlockSpec((n_back + 1, tq, tq), lambda b, g, i: (0, 0, 0))],
        out_specs=pl.BlockSpec((tq, width), lambda b, g, i: (b * nq + i, g)),
        out_shape=jax.ShapeDtypeStruct((batch * seq, n_heads * HEAD_DIM), BF16),
        scratch_shapes=[pltpu.VMEM((nh, seq, 2 * HEAD_DIM), BF16)] + _softmax_scratch(tq, nh),
        compiler_params=_params(("parallel", "parallel", "arbitrary"), 40),
        name="dilated_attention",
    )(qkv, qkv, qkv, lb)


def _block_mean_body(k_ref, o_ref):
    n = pl.program_id(1)
    mean = jnp.mean(k_ref[...].astype(F32), axis=0, keepdims=True)
    o_ref[0, pl.ds(n, 1), :] = mean


def moba_block_means(qkv, *, batch, seq, n_heads, k_col):
    nblk = seq // MOBA_BLOCK
    width = n_heads * HEAD_DIM
    kc = k_col * HEAD_DIM // width
    return pl.pallas_call(
        _block_mean_body,
        grid=(batch, nblk),
        in_specs=[pl.BlockSpec((MOBA_BLOCK, width), lambda b, n: (b * nblk + n, kc))],
        out_specs=pl.BlockSpec((1, nblk, width), lambda b, n: (b, 0, 0)),
        out_shape=jax.ShapeDtypeStruct((batch, nblk, width), F32),
        compiler_params=_params(("parallel", "arbitrary"), 16),
        name="moba_block_means",
    )(qkv)


def _moba_select_bias(q, kbar, own, nblk):
    kbar = jnp.concatenate([kbar, jnp.zeros((LANES - nblk, HEAD_DIM), F32)], axis=0)
    kb_hi = kbar.astype(BF16)
    kb_lo = (kbar - kb_hi.astype(F32)).astype(BF16)
    gate = _dot_nt(q, kb_hi) + _dot_nt(q, kb_lo)
    lane = lax.broadcasted_iota(jnp.int32, gate.shape, 1)
    lane_f = lane.astype(F32)
    past = lane < own
    g = jnp.where(past, gate, -jnp.inf)
    selected = jnp.zeros(gate.shape, jnp.bool_)
    for _ in range(MOBA_TOPK):
        mx = jnp.max(g, axis=1, keepdims=True)
        first = jnp.min(jnp.where(g == mx, lane_f, float(LANES)), axis=1, keepdims=True)
        pick = (lane_f == first) & past
        selected = selected | pick
        g = jnp.where(pick, -jnp.inf, g)
    return jnp.where(selected | (lane == own), 0.0, NEG)


def _moba_body(q_ref, k_ref, v_ref, kbar_ref, onehot_ref, o_ref, qa_ref, ka_ref, va_ref, *scratch, nblk):
    nh = HEADS_PER_STEP
    m_refs, acc_refs = scratch[:nh], scratch[nh:]
    i = pl.program_id(2)
    tq = q_ref.shape[0]

    @pl.when(i == 0)
    def _():
        _fill_value_scratch(va_ref, v_ref)
        for a in range(nh):
            ka_ref[a, :, :HEAD_DIM] = k_ref[:, _head_cols(a)]
            ka_ref[a, :, HEAD_DIM:] = onehot_ref[...]

    row = lax.broadcasted_iota(jnp.int32, (tq, LANES), 0)
    own = (i * tq + row) // MOBA_BLOCK
    for a in range(nh):
        q = q_ref[:, _head_cols(a)]
        qa_ref[a, :, :HEAD_DIM] = q
        qa_ref[a, :, HEAD_DIM:] = _moba_select_bias(q, kbar_ref[0, :, _head_cols(a)], own, nblk).astype(BF16)

    _causal_flash(i, qa_ref, ka_ref, va_ref, m_refs, acc_refs, o_ref)


def moba_attention(qkv, kbar, *, batch, seq, n_heads, q_col, k_col, v_col, tq):
    nblk = seq // MOBA_BLOCK
    nq = seq // tq
    nh = HEADS_PER_STEP
    groups = n_heads // nh
    width = nh * HEAD_DIM
    qg, kg, vg = q_col // nh, k_col // nh, v_col // nh
    onehot = (np.arange(seq)[:, None] // MOBA_BLOCK == np.arange(LANES)[None, :]).astype(np.float32)
    return pl.pallas_call(
        functools.partial(_moba_body, nblk=nblk),
        grid=(batch, groups, nq),
        in_specs=[pl.BlockSpec((tq, width), lambda b, g, i: (b * nq + i, qg + g)),
                  pl.BlockSpec((seq, width), lambda b, g, i: (b, kg + g)),
                  pl.BlockSpec((seq, width), lambda b, g, i: (b, vg + g)),
                  pl.BlockSpec((1, nblk, width), lambda b, g, i: (b, 0, g)),
                  pl.BlockSpec((seq, LANES), lambda b, g, i: (0, 0))],
        out_specs=pl.BlockSpec((tq, width), lambda b, g, i: (b * nq + i, g)),
        out_shape=jax.ShapeDtypeStruct((batch * seq, n_heads * HEAD_DIM), BF16),
        scratch_shapes=[pltpu.VMEM((nh, tq, 2 * HEAD_DIM), BF16),
                        pltpu.VMEM((nh, seq, 2 * HEAD_DIM), BF16),
                        pltpu.VMEM((nh, seq, 2 * HEAD_DIM), BF16)] + _softmax_scratch(tq, nh),
        compiler_params=_params(("parallel", "parallel", "arbitrary"), 40),
        name="moba_attention",
    )(qkv, qkv, qkv, kbar, jnp.asarray(onehot, BF16))


def _fox_gate_body(x_ref, g_ref, wf_ref, bf_ref, tri_ref, c_ref, carry_ref):
    @pl.when(pl.program_id(1) == 0)
    def _():
        carry_ref[...] = jnp.zeros(carry_ref.shape, F32)

    h = _rms(x_ref[...], g_ref[...]).astype(BF16)
    fg = _dot(h, wf_ref[...]) + bf_ref[...]
    logf = (jnp.minimum(fg, 0.0) - jnp.log(1.0 + jnp.exp(-jnp.abs(fg)))) * LOG2E
    hi = logf.astype(BF16)
    rem = logf - hi.astype(F32)
    mid = rem.astype(BF16)
    lo = (rem - mid.astype(F32)).astype(BF16)
    tri = tri_ref[...]
    c = _dot(tri, hi) + _dot(tri, mid) + _dot(tri, lo) + carry_ref[0:1, :]
    c_ref[...] = c
    carry_ref[...] = jnp.broadcast_to(c[c.shape[0] - 1:, :], carry_ref.shape)


def fox_gate_cumsum(x, g, wf, bf, *, batch, seq, tm):
    T, D = x.shape
    ns = seq // tm
    tri = jnp.asarray(np.tril(np.ones((tm, tm), np.float32)), BF16)
    return pl.pallas_call(
        _fox_gate_body,
        grid=(batch, ns),
        in_specs=[pl.BlockSpec((tm, D), lambda b, s: (b * ns + s, 0)),
                  pl.BlockSpec((1, D), lambda b, s: (0, 0)),
                  pl.BlockSpec((D, LANES), lambda b, s: (0, 0)),
                  pl.BlockSpec((1, LANES), lambda b, s: (0, 0)),
                  pl.BlockSpec((tm, tm), lambda b, s: (0, 0))],
        out_specs=pl.BlockSpec((tm, LANES), lambda b, s: (b * ns + s, 0)),
        out_shape=jax.ShapeDtypeStruct((T, LANES), F32),
        scratch_shapes=[pltpu.VMEM((SUBLANES, LANES), F32)],
        compiler_params=_params(("parallel", "arbitrary"), 32),
        name="fox_gate_cumsum",
    )(x, g.reshape(1, D), wf, bf, tri)


def _split3(c):
    hi = c.astype(BF16).astype(F32)
    mid = (c - hi).astype(BF16).astype(F32)
    lo = (c - hi - mid).astype(BF16).astype(F32)
    return hi, mid, lo


def _fox_bias_factors(cc, head, negate):
    lane = lax.broadcasted_iota(jnp.int32, cc.shape, 1)
    c = jnp.sum(jnp.where(lane == head, cc, 0.0), axis=1, keepdims=True)
    hi, mid, lo = _split3(-c if negate else c)
    base = 3 if negate else 0
    terms = jnp.where(lane == base, hi, jnp.where(lane == base + 1, mid, lo))
    is_term = (lane >= base) & (lane < base + 3)
    return jnp.where(is_term, terms, jnp.where(lane < 6, 1.0, 0.0))


def _fox_body(q_ref, k_ref, v_ref, c_ref, o_ref, qa_ref, ka_ref, va_ref, *scratch):
    nh = HEADS_PER_STEP
    m_refs, acc_refs = scratch[:nh], scratch[nh:]
    g = pl.program_id(1)
    i = pl.program_id(2)
    tq = q_ref.shape[0]

    @pl.when(i == 0)
    def _():
        _fill_value_scratch(va_ref, v_ref)
        cc = c_ref[...]
        for a in range(nh):
            ka_ref[a, :, :HEAD_DIM] = k_ref[:, _head_cols(a)]
            ka_ref[a, :, HEAD_DIM:] = _fox_bias_factors(cc, g * nh + a, True).astype(BF16)

    cc = c_ref[pl.ds(pl.multiple_of(i * tq, tq), tq), :]
    for a in range(nh):
        qa_ref[a, :, :HEAD_DIM] = q_ref[:, _head_cols(a)]
        qa_ref[a, :, HEAD_DIM:] = _fox_bias_factors(cc, g * nh + a, False).astype(BF16)

    _causal_flash(i, qa_ref, ka_ref, va_ref, m_refs, acc_refs, o_ref)


def fox_attention(qkv, c_col, *, batch, seq, n_heads, tq):
    nq = seq // tq
    nh = HEADS_PER_STEP
    groups = n_heads // nh
    width = nh * HEAD_DIM
    return pl.pallas_call(
        _fox_body,
        grid=(batch, groups, nq),
        in_specs=[pl.BlockSpec((tq, width), lambda b, g, i: (b * nq + i, g)),
                  pl.BlockSpec((seq, width), lambda b, g, i: (b, groups + g)),
                  pl.BlockSpec((seq, width), lambda b, g, i: (b, 2 * groups + g)),
                  pl.BlockSpec((seq, LANES), lambda b, g, i: (b, 0))],
        out_specs=pl.BlockSpec((tq, width), lambda b, g, i: (b * nq + i, g)),
        out_shape=jax.ShapeDtypeStruct((batch * seq, n_heads * HEAD_DIM), BF16),
        scratch_shapes=[pltpu.VMEM((nh, tq, 2 * HEAD_DIM), BF16),
                        pltpu.VMEM((nh, seq, 2 * HEAD_DIM), BF16),
                        pltpu.VMEM((nh, seq, 2 * HEAD_DIM), BF16)] + _softmax_scratch(tq, nh),
        compiler_params=_params(("parallel", "parallel", "arbitrary"), 40),
        name="fox_attention",
    )(qkv, qkv, qkv, c_col)


def _mem_attn_body(x_ref, gpre_ref, wq_ref, kv_ref, wo_ref, gpost_ref, o_ref):
    x = x_ref[...]
    h = _rms(x, gpre_ref[...]).astype(BF16)
    q = _dot(h, wq_ref[...]).astype(BF16)
    width = MEM_HEADS * MEM_HEAD_DIM
    outs = []
    for hd in range(MEM_HEADS):
        lo, hi = hd * MEM_HEAD_DIM, (hd + 1) * MEM_HEAD_DIM
        s = _dot_nt(q[:, lo:hi], kv_ref[:, lo:hi])
        p = jnp.exp(s - jnp.max(s, axis=1, keepdims=True))
        o = _dot(p.astype(BF16), kv_ref[:, width + lo:width + hi])
        outs.append((o / jnp.sum(p, axis=1, keepdims=True)).astype(BF16))
    y = _dot(jnp.concatenate(outs, axis=1), wo_ref[...])
    o_ref[...] = x + _rms(y, gpost_ref[...])


def mem_attention(x, gpre, wq, kv, wo, gpost, *, seq, mem_tokens, tm):
    T, D = x.shape
    width = wq.shape[1]
    per_seq = seq // tm
    return pl.pallas_call(
        _mem_attn_body,
        grid=(T // tm,),
        in_specs=[pl.BlockSpec((tm, D), lambda i: (i, 0)),
                  pl.BlockSpec((1, D), lambda i: (0, 0)),
                  pl.BlockSpec((D, width), lambda i: (0, 0)),
                  pl.BlockSpec((mem_tokens, 2 * width), lambda i: (i // per_seq, 0)),
                  pl.BlockSpec((width, D), lambda i: (0, 0)),
                  pl.BlockSpec((1, D), lambda i: (0, 0))],
        out_specs=pl.BlockSpec((tm, D), lambda i: (i, 0)),
        out_shape=jax.ShapeDtypeStruct((T, D), F32),
        compiler_params=_params(("parallel",), 48),
        name="mem_attention",
    )(x, gpre.reshape(1, D), wq, kv, wo, gpost.reshape(1, D))


def _ffn_body(x_ref, gpre_ref, wg_ref, wv_ref, cwg_ref, cwv_ref, cbg_ref, cbv_ref, wd_ref, gpost_ref,
              o_ref, h_ref, acc_ref, ubuf_ref, carry_ref, act0_ref, act1_ref, *, tiles_per_seq, nf):
    i = pl.program_id(0)
    f = pl.program_id(1)
    tm = x_ref.shape[0]
    halo = SUBLANES
    acts = (act0_ref, act1_ref)
    seq_start = i % tiles_per_seq == 0

    def conv_branch(h, w_ref, cw_ref, cb_ref, slot):
        u = _dot(h, w_ref[...])
        ubuf_ref[slot, 0:halo, :] = jnp.where(seq_start, 0.0, carry_ref[f, slot])
        ubuf_ref[slot, halo:, :] = u
        carry_ref[f, slot] = u[tm - halo:, :]
        cw = cw_ref[...]
        return (cb_ref[...] + cw[0:1] * ubuf_ref[slot, halo - 2:halo - 2 + tm, :]
                + cw[1:2] * ubuf_ref[slot, halo - 1:halo - 1 + tm, :] + cw[2:3] * u)

    def activation():
        h = h_ref[...]
        g = conv_branch(h, wg_ref, cwg_ref, cbg_ref, 0)
        v = conv_branch(h, wv_ref, cwv_ref, cbv_ref, 1)
        return (g / (1.0 + jnp.exp(-g)) * v).astype(BF16)

    def down(act_ref):
        acc_ref[...] += _dot(act_ref[...], wd_ref[...])

    @pl.when(f == 0)
    def _():
        h_ref[...] = _rms(x_ref[...], gpre_ref[...]).astype(BF16)
        acc_ref[...] = jnp.zeros(acc_ref.shape, F32)
        act0_ref[...] = activation()

    for parity in range(2):
        @pl.when((f > 0) & (f < nf) & (f % 2 == parity))
        def _(parity=parity):
            down(acts[1 - parity])
            acts[parity][...] = activation()

    @pl.when(f == nf)
    def _():
        down(acts[(nf - 1) % 2])
        o_ref[...] = x_ref[...] + _rms(acc_ref[...], gpost_ref[...])


def conv_ffn(x, gpre, w_up, conv_w, conv_b, w_down, gpost, *, seq, tm, tf):
    T, D = x.shape
    dff = w_down.shape[0]
    nf = dff // tf
    cb = conv_b.reshape(1, 2 * dff)

    def up(f):
        return jnp.minimum(f, nf - 1)

    def dn(f):
        return jnp.maximum(f - 1, 0)

    return pl.pallas_call(
        functools.partial(_ffn_body, tiles_per_seq=seq // tm, nf=nf),
        grid=(T // tm, nf + 1),
        in_specs=[pl.BlockSpec((tm, D), lambda i, f: (i, 0)),
                  pl.BlockSpec((1, D), lambda i, f: (0, 0)),
                  pl.BlockSpec((D, tf), lambda i, f: (0, up(f))),
                  pl.BlockSpec((D, tf), lambda i, f: (0, up(f) + nf)),
                  pl.BlockSpec((CONV_WIDTH, tf), lambda i, f: (0, up(f))),
                  pl.BlockSpec((CONV_WIDTH, tf), lambda i, f: (0, up(f) + nf)),
                  pl.BlockSpec((1, tf), lambda i, f: (0, up(f))),
                  pl.BlockSpec((1, tf), lambda i, f: (0, up(f) + nf)),
                  pl.BlockSpec((tf, D), lambda i, f: (dn(f), 0)),
                  pl.BlockSpec((1, D), lambda i, f: (0, 0))],
        out_specs=pl.BlockSpec((tm, D), lambda i, f: (i, 0)),
        out_shape=jax.ShapeDtypeStruct((T, D), F32),
        scratch_shapes=[pltpu.VMEM((tm, D), BF16),
                        pltpu.VMEM((tm, D), F32),
                        pltpu.VMEM((2, SUBLANES + tm, tf), F32),
                        pltpu.VMEM((nf, 2, SUBLANES, tf), F32),
                        pltpu.VMEM((tm, tf), BF16),
                        pltpu.VMEM((tm, tf), BF16)],
        compiler_params=_params(("arbitrary", "arbitrary"), 56),
        name="conv_ffn",
    )(x, gpre.reshape(1, D), w_up, w_up, conv_w, conv_w, cb, cb, w_down, gpost.reshape(1, D))


def _rope_head_permutation():
    half = ROT_DIM // 2
    mid = HEAD_DIM // 2
    return np.concatenate([np.arange(0, half), np.arange(ROT_DIM, mid + half),
                           np.arange(half, ROT_DIM), np.arange(mid + half, HEAD_DIM)])


def _rope_tables(positions):
    half = ROT_DIM // 2
    mid = HEAD_DIM // 2
    inv_freq = ROPE_THETA ** (-jnp.arange(0, ROT_DIM, 2, dtype=F32) / ROT_DIM)
    ang = positions.astype(F32).reshape(-1, 1) * inv_freq
    cos, sin = jnp.cos(ang), jnp.sin(ang)
    T = ang.shape[0]
    ones = jnp.ones((T, mid - half), F32)
    zeros = jnp.zeros((T, mid - half), F32)
    c = jnp.concatenate([cos, ones, cos, ones], axis=1)
    s = jnp.concatenate([-sin, zeros, sin, zeros], axis=1)
    return jnp.stack([c, jnp.ones_like(c)]), jnp.stack([s, jnp.zeros_like(s)])


def _even_mixer(x, g_pre, w_in, w_out, g_post, rope, *, batch, seq):
    scale = HEAD_DIM ** -0.5 * LOG2E
    part = N_HEADS_A * HEAD_DIM
    half, mid = ROT_DIM // 2, HEAD_DIM // 2
    w5 = w_in.reshape(D_MODEL, 2, 3, N_HEADS_A, HEAD_DIM) * jnp.asarray([scale, 1.0, 1.0], F32).reshape(3, 1, 1)
    qk = w5[:, :, :2]
    qk = jnp.concatenate([qk[..., :half], qk[..., ROT_DIM:mid + half], qk[..., half:ROT_DIM],
                          qk[..., mid + half:]], axis=-1)
    w = jnp.concatenate([qk, w5[:, :, 2:]], axis=2).astype(BF16).reshape(D_MODEL, 6 * part)
    qkv = norm_proj(x, g_pre, w, tm=512, tn=part, out_dtype=BF16, rope=rope, rope_period=3)
    hb = N_HEADS_A
    oa = dilated_attention(qkv, batch=batch, seq=seq, n_heads=N_HEADS_A,
                           q_col=0, k_col=hb, v_col=2 * hb, tq=512)
    kbar = moba_block_means(qkv, batch=batch, seq=seq, n_heads=N_HEADS_B, k_col=4 * hb)
    ob = moba_attention(qkv, kbar, batch=batch, seq=seq, n_heads=N_HEADS_B,
                        q_col=3 * hb, k_col=4 * hb, v_col=5 * hb, tq=512)
    return proj_norm_res([oa, ob], w_out.astype(BF16), x, g_post, tm=512)


def _odd_mixer(x, g_pre, w_in, b_f, w_out, g_post, *, batch, seq):
    scale = HEAD_DIM ** -0.5 * LOG2E
    nqkv = 3 * D_MODEL
    col_scale = jnp.concatenate([jnp.full((1, D_MODEL), scale, F32), jnp.ones((1, 2 * D_MODEL), F32)], axis=1)
    w = (w_in[:, :nqkv] * col_scale).astype(BF16)
    qkv = norm_proj(x, g_pre, w, tm=512, tn=1024, out_dtype=BF16)
    wf = jnp.pad(w_in[:, nqkv:], ((0, 0), (0, LANES - N_HEADS))).astype(BF16)
    bf = jnp.pad(b_f.astype(F32), (0, LANES - N_HEADS)).reshape(1, LANES)
    c_col = fox_gate_cumsum(x, g_pre, wf, bf, batch=batch, seq=seq, tm=512)
    o = fox_attention(qkv, c_col, batch=batch, seq=seq, n_heads=N_HEADS, tq=512)
    return proj_norm_res([o], w_out.astype(BF16), x, g_post, tm=512)


def kernel(x, mem, positions, w_in_ab, w_out_ab, w_in_c, b_f, w_out_c, g_mix_pre, g_mix_post, g_mem_kv, g_mem_pre, g_mem_post, w_mq, w_mk, w_mv, w_mo, g_ffn_pre, g_ffn_post, w_up, conv_w, conv_b, w_down):
    batch, seq, d = x.shape
    mem_tokens = mem.shape[1]
    depth = g_mix_pre.shape[0]
    rope = _rope_tables(positions)
    xs = x.reshape(batch * seq, d)
    mems = mem.reshape(batch * mem_tokens, d)
    mem_scale = MEM_HEAD_DIM ** -0.5
    for layer in range(depth):
        if layer % 2 == 0:
            xs = _even_mixer(xs, g_mix_pre[layer], w_in_ab[layer // 2], w_out_ab[layer // 2],
                             g_mix_post[layer], rope, batch=batch, seq=seq)
        else:
            xs = _odd_mixer(xs, g_mix_pre[layer], w_in_c[layer // 2], b_f[layer // 2], w_out_c[layer // 2],
                            g_mix_post[layer], batch=batch, seq=seq)
        w_kv = jnp.concatenate([w_mk[layer], w_mv[layer]], axis=1).astype(BF16)
        kv = norm_proj(mems, g_mem_kv[layer], w_kv, tm=512, tn=w_kv.shape[1], out_dtype=BF16)
        xs = mem_attention(xs, g_mem_pre[layer], (w_mq[layer] * mem_scale).astype(BF16), kv,
                           w_mo[layer].astype(BF16), g_mem_post[layer],
                           seq=seq, mem_tokens=mem_tokens, tm=512)
        xs = conv_ffn(xs, g_ffn_pre[layer], w_up[layer].astype(BF16), conv_w[layer], conv_b[layer],
                      w_down[layer].astype(BF16), g_ffn_post[layer], seq=seq, tm=512, tf=512)
    return xs.reshape(batch, seq, d)
```

```python
import functools

import numpy as np
import jax
import jax.numpy as jnp
from jax import lax
from jax.experimental import pallas as pl
from jax.experimental.pallas import tpu as pltpu

D_MODEL = 2048
HEAD_DIM = 128
N_HEADS = D_MODEL // HEAD_DIM
N_HEADS_A = N_HEADS // 2
N_HEADS_B = N_HEADS - N_HEADS_A
DILATED_CONFIGS = ((128, 1), (512, 4), (2048, 16))
MOBA_BLOCK = 256
MOBA_TOPK = 3
ROT_DIM = HEAD_DIM // 4
ROPE_THETA = 500000.0
MEM_HEADS = 4
MEM_HEAD_DIM = 128
D_FF = ((8 * D_MODEL // 3 + 255) // 256) * 256
CONV_WIDTH = 3
RMS_EPS = 1e-6

LANES = 128
SUBLANES = 8
NEG = -1e30
LOG2E = 1.4426950408889634
HEADS_PER_STEP = 2
F32 = jnp.float32
BF16 = jnp.bfloat16
MIB = 1024 * 1024


def _params(semantics, vmem_mib):
    return pltpu.CompilerParams(dimension_semantics=semantics,
                                vmem_limit_bytes=vmem_mib * MIB)


def _rms(x, g):
    ms = jnp.mean(x * x, axis=-1, keepdims=True)
    return x * lax.rsqrt(ms + RMS_EPS) * g


def _dot(a, b):
    return jnp.dot(a, b, preferred_element_type=F32)


def _dot_nt(a, b):
    return lax.dot_general(a, b, (((1,), (1,)), ((), ())), preferred_element_type=F32)


def _softmax_init(m_refs, acc_refs):
    for m_ref, acc_ref in zip(m_refs, acc_refs):
        m_ref[...] = jnp.full(m_ref.shape, NEG, F32)
        acc_ref[...] = jnp.zeros(acc_ref.shape, F32)


def _multi_softmax_step(s_list, v_list, m_refs, acc_refs):
    m_prev = [r[...] for r in m_refs]
    m_new = [jnp.maximum(mp, jnp.max(s, axis=1, keepdims=True)) for mp, s in zip(m_prev, s_list)]
    p_list = [jnp.exp2(s - _tile_lanes(mn, s.shape[1])) for s, mn in zip(s_list, m_new)]
    for a, p in enumerate(p_list):
        alpha = jnp.exp2(m_prev[a] - m_new[a])
        acc_ref = acc_refs[a]
        acc_ref[...] = _tile_lanes(alpha, acc_ref.shape[1]) * acc_ref[...] + _dot(p.astype(BF16), v_list[a])
        m_refs[a][...] = m_new[a]


def _softmax_finish(o_ref, acc_refs):
    for a, acc_ref in enumerate(acc_refs):
        acc = acc_ref[...]
        o_ref[:, _head_cols(a)] = (acc[:, :HEAD_DIM] / acc[:, HEAD_DIM:]).astype(o_ref.dtype)


def _tile_lanes(x, width):
    n = width // LANES
    return x if n == 1 else jnp.concatenate([x] * n, axis=1)


def _softmax_scratch(tq, nh):
    return [pltpu.VMEM((tq, LANES), F32)] * nh + [pltpu.VMEM((tq, 2 * HEAD_DIM), F32)] * nh


def _head_cols(a):
    return slice(a * HEAD_DIM, (a + 1) * HEAD_DIM)


def _fill_value_scratch(va_ref, v_ref):
    for a in range(va_ref.shape[0]):
        va_ref[a, :, :HEAD_DIM] = v_ref[:, _head_cols(a)]
        va_ref[a, :, HEAD_DIM:] = jnp.ones((va_ref.shape[1], HEAD_DIM), va_ref.dtype)


def _causal_flash(i, qa_ref, ka_ref, va_ref, m_refs, acc_refs, o_ref):
    nh, tq, _ = qa_ref.shape
    _softmax_init(m_refs, acc_refs)

    def step(first_tile, n_tiles, diagonal):
        start = pl.multiple_of(first_tile * tq, tq)
        width = n_tiles * tq
        s_list = [_dot_nt(qa_ref[a], ka_ref[a, pl.ds(start, width), :]) for a in range(nh)]
        v_list = [va_ref[a, pl.ds(start, width), :] for a in range(nh)]
        if diagonal:
            row = lax.broadcasted_iota(jnp.int32, (tq, width), 0)
            col = lax.broadcasted_iota(jnp.int32, (tq, width), 1)
            s_list = [jnp.where(col - (width - tq) <= row, s, NEG) for s in s_list]
        _multi_softmax_step(s_list, v_list, m_refs, acc_refs)

    def below_diagonal_pair(jj, carry):
        step(2 * jj, 2, False)
        return carry

    lax.fori_loop(0, i // 2, below_diagonal_pair, 0)

    @pl.when(i % 2 == 1)
    def _():
        step(i - 1, 2, True)

    @pl.when(i % 2 == 0)
    def _():
        step(i, 1, True)

    _softmax_finish(o_ref, acc_refs)


def _norm_proj_body(x_ref, g_ref, w_ref, o_ref, h_ref):
    @pl.when(pl.program_id(1) == 0)
    def _():
        h_ref[...] = _rms(x_ref[...], g_ref[...]).astype(BF16)

    o_ref[...] = _dot(h_ref[...], w_ref[...]).astype(o_ref.dtype)


def _norm_proj_rope_body(x_ref, g_ref, w_ref, cos_ref, sin_ref, o_ref, h_ref):
    @pl.when(pl.program_id(1) == 0)
    def _():
        h_ref[...] = _rms(x_ref[...], g_ref[...]).astype(BF16)

    acc = _dot(h_ref[...], w_ref[...])
    c, s = cos_ref[0], sin_ref[0]
    for hh in range(acc.shape[1] // HEAD_DIM):
        t = acc[:, hh * HEAD_DIM:(hh + 1) * HEAD_DIM]
        r = t * c + pltpu.roll(t, HEAD_DIM // 2, 1) * s
        o_ref[:, hh * HEAD_DIM:(hh + 1) * HEAD_DIM] = r.astype(o_ref.dtype)


def norm_proj(x, g, w, layer, *, tm, tn, out_dtype, rope=None, rope_period=None):
    T, D = x.shape
    N = w.shape[2]
    grid = (T // tm, N // tn)
    in_specs = [pl.BlockSpec((tm, D), lambda i, j: (i, 0)),
                pl.BlockSpec((1, D), lambda i, j: (0, 0)),
                pl.BlockSpec((None, D, tn), lambda i, j: (layer, 0, j))]
    args = [x, g.reshape(1, D), w]
    if rope is None:
        body = _norm_proj_body
    else:
        body = _norm_proj_rope_body
        table_spec = pl.BlockSpec((1, tm, HEAD_DIM),
                                  lambda i, j: ((j % rope_period) // (rope_period - 1), i, 0))
        in_specs += [table_spec] * 2
        args += list(rope)
    return pl.pallas_call(
        body,
        grid=grid,
        in_specs=in_specs,
        out_specs=pl.BlockSpec((tm, tn), lambda i, j: (i, j)),
        out_shape=jax.ShapeDtypeStruct((T, N), out_dtype),
        scratch_shapes=[pltpu.VMEM((tm, D), BF16)],
        compiler_params=_params(("parallel", "arbitrary"), 48),
        name="norm_proj",
    )(*args)


def _proj_norm_res_body(*refs):
    *a_refs, w_ref, x_ref, g_ref, o_ref = refs
    y, row = None, 0
    for a_ref in a_refs:
        k = a_ref.shape[1]
        part = _dot(a_ref[...], w_ref[row:row + k, :])
        y = part if y is None else y + part
        row += k
    o_ref[...] = x_ref[...] + _rms(y, g_ref[...])


def proj_norm_res(a_parts, w, layer, x, g, *, tm):
    T = x.shape[0]
    _, K, D = w.shape
    return pl.pallas_call(
        _proj_norm_res_body,
        grid=(T // tm,),
        in_specs=[pl.BlockSpec((tm, a.shape[1]), lambda i: (i, 0)) for a in a_parts]
                 + [pl.BlockSpec((None, K, D), lambda i: (layer, 0, 0)),
                    pl.BlockSpec((tm, D), lambda i: (i, 0)),
                    pl.BlockSpec((1, D), lambda i: (0, 0))],
        out_specs=pl.BlockSpec((tm, D), lambda i: (i, 0)),
        out_shape=jax.ShapeDtypeStruct((T, D), F32),
        compiler_params=_params(("parallel",), 48),
        name="proj_norm_res",
    )(*a_parts, w, x, g.reshape(1, D))


def _dilated_log_multiplicity(tq, n_back):
    r = np.arange(tq)[:, None]
    c = np.arange(tq)[None, :]
    out = []
    for j in range(n_back + 1):
        d = (n_back - j) * tq + r - c
        mult = np.zeros((tq, tq), np.float64)
        for window, dil in DILATED_CONFIGS:
            mult += (d >= 0) & (d % dil == 0) & (d <= window)
        out.append(np.where(mult > 0, np.log2(np.maximum(mult, 1.0)), NEG))
    return np.stack(out).astype(np.float32)


def _dilated_body(q_ref, k_ref, v_ref, lb_ref, o_ref, va_ref, *scratch, n_back):
    nh = HEADS_PER_STEP
    m_refs, acc_refs = scratch[:nh], scratch[nh:]
    i = pl.program_id(2)
    tq = q_ref.shape[0]

    @pl.when(i == 0)
    def _():
        _fill_value_scratch(va_ref, v_ref)

    _softmax_init(m_refs, acc_refs)

    def step(first_j, n_tiles):
        start = pl.multiple_of((i - n_back + first_j) * tq, tq)
        width = n_tiles * tq
        lb = [lb_ref[first_j + t] for t in range(n_tiles)]
        lb = lb[0] if n_tiles == 1 else jnp.concatenate(lb, axis=1)
        s_list = [_dot_nt(q_ref[:, _head_cols(a)], k_ref[pl.ds(start, width), _head_cols(a)]) + lb
                  for a in range(nh)]
        v_list = [va_ref[a, pl.ds(start, width), :] for a in range(nh)]
        _multi_softmax_step(s_list, v_list, m_refs, acc_refs)

    total = jnp.minimum(i, n_back) + 1
    j_lo = n_back + 1 - total

    @pl.when(total % 2 == 1)
    def _():
        step(j_lo, 1)

    def pair(p, carry):
        step(j_lo + total % 2 + 2 * p, 2)
        return carry

    lax.fori_loop(0, total // 2, pair, 0)
    _softmax_finish(o_ref, acc_refs)


def dilated_attention(qkv, *, batch, seq, n_heads, q_col, k_col, v_col, tq):
    nq = seq // tq
    nh = HEADS_PER_STEP
    groups = n_heads // nh
    width = nh * HEAD_DIM
    max_window = max(w for w, _ in DILATED_CONFIGS)
    n_back = -(-max_window // tq)
    lb = jnp.asarray(_dilated_log_multiplicity(tq, n_back))
    qg, kg, vg = q_col // nh, k_col // nh, v_col // nh
    return pl.pallas_call(
        functools.partial(_dilated_body, n_back=n_back),
        grid=(batch, groups, nq),
        in_specs=[pl.BlockSpec((tq, width), lambda b, g, i: (b * nq + i, qg + g)),
                  pl.BlockSpec((seq, width), lambda b, g, i: (b, kg + g)),
                  pl.BlockSpec((seq, width), lambda b, g, i: (b, vg + g)),
                  pl.BlockSpec((n_back + 1, tq, tq), lambda b, g, i: (0, 0, 0))],
        out_specs=pl.BlockSpec((tq, width), lambda b, g, i: (b * nq + i, g)),
        out_shape=jax.ShapeDtypeStruct((batch * seq, n_heads * HEAD_DIM), BF16),
        scratch_shapes=[pltpu.VMEM((nh, seq, 2 * HEAD_DIM), BF16)] + _softmax_scratch(tq, nh),
        compiler_params=_params(("parallel", "parallel", "arbitrary"), 40),
        name="dilated_attention",
    )(qkv, qkv, qkv, lb)


def _block_mean_body(k_ref, o_ref):
    n = pl.program_id(1)
    mean = jnp.mean(k_ref[...].astype(F32), axis=0, keepdims=True)
    o_ref[0, pl.ds(n, 1), :] = mean


def moba_block_means(qkv, *, batch, seq, n_heads, k_col):
    nblk = seq // MOBA_BLOCK
    width = n_heads * HEAD_DIM
    kc = k_col * HEAD_DIM // width
    return pl.pallas_call(
        _block_mean_body,
        grid=(batch, nblk),
        in_specs=[pl.BlockSpec((MOBA_BLOCK, width), lambda b, n: (b * nblk + n, kc))],
        out_specs=pl.BlockSpec((1, nblk, width), lambda b, n: (b, 0, 0)),
        out_shape=jax.ShapeDtypeStruct((batch, nblk, width), F32),
        compiler_params=_params(("parallel", "arbitrary"), 16),
        name="moba_block_means",
    )(qkv)


def _moba_select_bias(q, kbar, own, nblk):
    kbar = jnp.concatenate([kbar, jnp.zeros((LANES - nblk, HEAD_DIM), F32)], axis=0)
    kb_hi = kbar.astype(BF16)
    kb_lo = (kbar - kb_hi.astype(F32)).astype(BF16)
    gate = _dot_nt(q, kb_hi) + _dot_nt(q, kb_lo)
    lane = lax.broadcasted_iota(jnp.int32, gate.shape, 1)
    lane_f = lane.astype(F32)
    past = lane < own
    g = jnp.where(past, gate, -jnp.inf)
    selected = jnp.zeros(gate.shape, jnp.bool_)
    for _ in range(MOBA_TOPK):
        mx = jnp.max(g, axis=1, keepdims=True)
        first = jnp.min(jnp.where(g == mx, lane_f, float(LANES)), axis=1, keepdims=True)
        pick = (lane_f == first) & past
        selected = selected | pick
        g = jnp.where(pick, -jnp.inf, g)
    return jnp.where(selected | (lane == own), 0.0, NEG)


def _moba_body(q_ref, k_ref, v_ref, kbar_ref, onehot_ref, o_ref, qa_ref, ka_ref, va_ref, *scratch, nblk):
    nh = HEADS_PER_STEP
    m_refs, acc_refs = scratch[:nh], scratch[nh:]
    i = pl.program_id(2)
    tq = q_ref.shape[0]

    @pl.when(i == 0)
    def _():
        _fill_value_scratch(va_ref, v_ref)
        for a in range(nh):
            ka_ref[a, :, :HEAD_DIM] = k_ref[:, _head_cols(a)]
            ka_ref[a, :, HEAD_DIM:] = onehot_ref[...]

    row = lax.broadcasted_iota(jnp.int32, (tq, LANES), 0)
    own = (i * tq + row) // MOBA_BLOCK
    for a in range(nh):
        q = q_ref[:, _head_cols(a)]
        qa_ref[a, :, :HEAD_DIM] = q
        qa_ref[a, :, HEAD_DIM:] = _moba_select_bias(q, kbar_ref[0, :, _head_cols(a)], own, nblk).astype(BF16)

    _causal_flash(i, qa_ref, ka_ref, va_ref, m_refs, acc_refs, o_ref)


def moba_attention(qkv, kbar, *, batch, seq, n_heads, q_col, k_col, v_col, tq):
    nblk = seq // MOBA_BLOCK
    nq = seq // tq
    nh = HEADS_PER_STEP
    groups = n_heads // nh
    width = nh * HEAD_DIM
    qg, kg, vg = q_col // nh, k_col // nh, v_col // nh
    onehot = (np.arange(seq)[:, None] // MOBA_BLOCK == np.arange(LANES)[None, :]).astype(np.float32)
    return pl.pallas_call(
        functools.partial(_moba_body, nblk=nblk),
        grid=(batch, groups, nq),
        in_specs=[pl.BlockSpec((tq, width), lambda b, g, i: (b * nq + i, qg + g)),
                  pl.BlockSpec((seq, width), lambda b, g, i: (b, kg + g)),
                  pl.BlockSpec((seq, width), lambda b, g, i: (b, vg + g)),
                  pl.BlockSpec((1, nblk, width), lambda b, g, i: (b, 0, g)),
                  pl.BlockSpec((seq, LANES), lambda b, g, i: (0, 0))],
        out_specs=pl.BlockSpec((tq, width), lambda b, g, i: (b * nq + i, g)),
        out_shape=jax.ShapeDtypeStruct((batch * seq, n_heads * HEAD_DIM), BF16),
        scratch_shapes=[pltpu.VMEM((nh, tq, 2 * HEAD_DIM), BF16),
                        pltpu.VMEM((nh, seq, 2 * HEAD_DIM), BF16),
                        pltpu.VMEM((nh, seq, 2 * HEAD_DIM), BF16)] + _softmax_scratch(tq, nh),
        compiler_params=_params(("parallel", "parallel", "arbitrary"), 40),
        name="moba_attention",
    )(qkv, qkv, qkv, kbar, jnp.asarray(onehot, BF16))


def _fox_gate_body(x_ref, g_ref, wf_ref, bf_ref, tri_ref, c_ref, carry_ref):
    @pl.when(pl.program_id(1) == 0)
    def _():
        carry_ref[...] = jnp.zeros(carry_ref.shape, F32)

    h = _rms(x_ref[...], g_ref[...]).astype(BF16)
    fg = _dot(h, wf_ref[...]) + bf_ref[...]
    logf = (jnp.minimum(fg, 0.0) - jnp.log(1.0 + jnp.exp(-jnp.abs(fg)))) * LOG2E
    hi = logf.astype(BF16)
    rem = logf - hi.astype(F32)
    mid = rem.astype(BF16)
    lo = (rem - mid.astype(F32)).astype(BF16)
    tri = tri_ref[...]
    c = _dot(tri, hi) + _dot(tri, mid) + _dot(tri, lo) + carry_ref[0:1, :]
    c_ref[...] = c
    carry_ref[...] = jnp.broadcast_to(c[c.shape[0] - 1:, :], carry_ref.shape)


def fox_gate_cumsum(x, g, wf, layer, bf, *, batch, seq, tm):
    T, D = x.shape
    ns = seq // tm
    tri = jnp.asarray(np.tril(np.ones((tm, tm), np.float32)), BF16)
    return pl.pallas_call(
        _fox_gate_body,
        grid=(batch, ns),
        in_specs=[pl.BlockSpec((tm, D), lambda b, s: (b * ns + s, 0)),
                  pl.BlockSpec((1, D), lambda b, s: (0, 0)),
                  pl.BlockSpec((None, D, LANES), lambda b, s: (layer, 0, 0)),
                  pl.BlockSpec((1, LANES), lambda b, s: (0, 0)),
                  pl.BlockSpec((tm, tm), lambda b, s: (0, 0))],
        out_specs=pl.BlockSpec((tm, LANES), lambda b, s: (b * ns + s, 0)),
        out_shape=jax.ShapeDtypeStruct((T, LANES), F32),
        scratch_shapes=[pltpu.VMEM((SUBLANES, LANES), F32)],
        compiler_params=_params(("parallel", "arbitrary"), 32),
        name="fox_gate_cumsum",
    )(x, g.reshape(1, D), wf, bf, tri)


def _split3(c):
    hi = c.astype(BF16).astype(F32)
    mid = (c - hi).astype(BF16).astype(F32)
    lo = (c - hi - mid).astype(BF16).astype(F32)
    return hi, mid, lo


def _fox_bias_factors(cc, head, negate):
    lane = lax.broadcasted_iota(jnp.int32, cc.shape, 1)
    c = jnp.sum(jnp.where(lane == head, cc, 0.0), axis=1, keepdims=True)
    hi, mid, lo = _split3(-c if negate else c)
    base = 3 if negate else 0
    terms = jnp.where(lane == base, hi, jnp.where(lane == base + 1, mid, lo))
    is_term = (lane >= base) & (lane < base + 3)
    return jnp.where(is_term, terms, jnp.where(lane < 6, 1.0, 0.0))


def _fox_body(q_ref, k_ref, v_ref, c_ref, o_ref, qa_ref, ka_ref, va_ref, *scratch):
    nh = HEADS_PER_STEP
    m_refs, acc_refs = scratch[:nh], scratch[nh:]
    g = pl.program_id(1)
    i = pl.program_id(2)
    tq = q_ref.shape[0]

    @pl.when(i == 0)
    def _():
        _fill_value_scratch(va_ref, v_ref)
        cc = c_ref[...]
        for a in range(nh):
            ka_ref[a, :, :HEAD_DIM] = k_ref[:, _head_cols(a)]
            ka_ref[a, :, HEAD_DIM:] = _fox_bias_factors(cc, g * nh + a, True).astype(BF16)

    cc = c_ref[pl.ds(pl.multiple_of(i * tq, tq), tq), :]
    for a in range(nh):
        qa_ref[a, :, :HEAD_DIM] = q_ref[:, _head_cols(a)]
        qa_ref[a, :, HEAD_DIM:] = _fox_bias_factors(cc, g * nh + a, False).astype(BF16)

    _causal_flash(i, qa_ref, ka_ref, va_ref, m_refs, acc_refs, o_ref)


def fox_attention(qkv, c_col, *, batch, seq, n_heads, tq):
    nq = seq // tq
    nh = HEADS_PER_STEP
    groups = n_heads // nh
    width = nh * HEAD_DIM
    return pl.pallas_call(
        _fox_body,
        grid=(batch, groups, nq),
        in_specs=[pl.BlockSpec((tq, width), lambda b, g, i: (b * nq + i, g)),
                  pl.BlockSpec((seq, width), lambda b, g, i: (b, groups + g)),
                  pl.BlockSpec((seq, width), lambda b, g, i: (b, 2 * groups + g)),
                  pl.BlockSpec((seq, LANES), lambda b, g, i: (b, 0))],
        out_specs=pl.BlockSpec((tq, width), lambda b, g, i: (b * nq + i, g)),
        out_shape=jax.ShapeDtypeStruct((batch * seq, n_heads * HEAD_DIM), BF16),
        scratch_shapes=[pltpu.VMEM((nh, tq, 2 * HEAD_DIM), BF16),
                        pltpu.VMEM((nh, seq, 2 * HEAD_DIM), BF16),
                        pltpu.VMEM((nh, seq, 2 * HEAD_DIM), BF16)] + _softmax_scratch(tq, nh),
        compiler_params=_params(("parallel", "parallel", "arbitrary"), 40),
        name="fox_attention",
    )(qkv, qkv, qkv, c_col)


def _mem_attn_body(x_ref, gpre_ref, wq_ref, kv_ref, wo_ref, gpost_ref, o_ref):
    x = x_ref[...]
    h = _rms(x, gpre_ref[...]).astype(BF16)
    q = _dot(h, wq_ref[...]).astype(BF16)
    width = MEM_HEADS * MEM_HEAD_DIM
    outs = []
    for hd in range(MEM_HEADS):
        lo, hi = hd * MEM_HEAD_DIM, (hd + 1) * MEM_HEAD_DIM
        s = _dot_nt(q[:, lo:hi], kv_ref[:, lo:hi])
        p = jnp.exp(s - jnp.max(s, axis=1, keepdims=True))
        o = _dot(p.astype(BF16), kv_ref[:, width + lo:width + hi])
        outs.append((o / jnp.sum(p, axis=1, keepdims=True)).astype(BF16))
    y = _dot(jnp.concatenate(outs, axis=1), wo_ref[...])
    o_ref[...] = x + _rms(y, gpost_ref[...])


def mem_attention(x, gpre, wq, kv, wo, layer, gpost, *, seq, mem_tokens, tm):
    T, D = x.shape
    width = wq.shape[2]
    per_seq = seq // tm
    return pl.pallas_call(
        _mem_attn_body,
        grid=(T // tm,),
        in_specs=[pl.BlockSpec((tm, D), lambda i: (i, 0)),
                  pl.BlockSpec((1, D), lambda i: (0, 0)),
                  pl.BlockSpec((None, D, width), lambda i: (layer, 0, 0)),
                  pl.BlockSpec((mem_tokens, 2 * width), lambda i: (i // per_seq, 0)),
                  pl.BlockSpec((None, width, D), lambda i: (layer, 0, 0)),
                  pl.BlockSpec((1, D), lambda i: (0, 0))],
        out_specs=pl.BlockSpec((tm, D), lambda i: (i, 0)),
        out_shape=jax.ShapeDtypeStruct((T, D), F32),
        compiler_params=_params(("parallel",), 48),
        name="mem_attention",
    )(x, gpre.reshape(1, D), wq, kv, wo, gpost.reshape(1, D))


def _ffn_body(x_ref, gpre_ref, wg_ref, wv_ref, cwg_ref, cwv_ref, cbg_ref, cbv_ref, wd_ref, gpost_ref,
              o_ref, h_ref, acc_ref, ubuf_ref, carry_ref, *, tiles_per_seq):
    i = pl.program_id(0)
    f = pl.program_id(1)
    tm = x_ref.shape[0]
    halo = SUBLANES

    @pl.when(f == 0)
    def _():
        h_ref[...] = _rms(x_ref[...], gpre_ref[...]).astype(BF16)
        acc_ref[...] = jnp.zeros(acc_ref.shape, F32)

    h = h_ref[...]
    seq_start = i % tiles_per_seq == 0

    def conv_branch(w_ref, cw_ref, cb_ref, slot):
        u = _dot(h, w_ref[...])
        ubuf_ref[slot, 0:halo, :] = jnp.where(seq_start, 0.0, carry_ref[f, slot])
        ubuf_ref[slot, halo:, :] = u
        carry_ref[f, slot] = u[tm - halo:, :]
        cw = cw_ref[...]
        return (cb_ref[...] + cw[0:1] * ubuf_ref[slot, halo - 2:halo - 2 + tm, :]
                + cw[1:2] * ubuf_ref[slot, halo - 1:halo - 1 + tm, :] + cw[2:3] * u)

    g = conv_branch(wg_ref, cwg_ref, cbg_ref, 0)
    v = conv_branch(wv_ref, cwv_ref, cbv_ref, 1)
    a = (g / (1.0 + jnp.exp(-g)) * v).astype(BF16)
    acc_ref[...] += _dot(a, wd_ref[...])

    @pl.when(f == pl.num_programs(1) - 1)
    def _():
        o_ref[...] = x_ref[...] + _rms(acc_ref[...], gpost_ref[...])


def conv_ffn(x, gpre, w_up, conv_w, conv_b, w_down, layer, gpost, *, seq, tm, tf):
    T, D = x.shape
    dff = w_down.shape[1]
    nf = dff // tf
    return pl.pallas_call(
        functools.partial(_ffn_body, tiles_per_seq=seq // tm),
        grid=(T // tm, nf),
        in_specs=[pl.BlockSpec((tm, D), lambda i, f: (i, 0)),
                  pl.BlockSpec((1, D), lambda i, f: (0, 0)),
                  pl.BlockSpec((None, D, tf), lambda i, f: (layer, 0, f)),
                  pl.BlockSpec((None, D, tf), lambda i, f: (layer, 0, f + nf)),
                  pl.BlockSpec((None, CONV_WIDTH, tf), lambda i, f: (layer, 0, f)),
                  pl.BlockSpec((None, CONV_WIDTH, tf), lambda i, f: (layer, 0, f + nf)),
                  pl.BlockSpec((None, 1, tf), lambda i, f: (layer, 0, f)),
                  pl.BlockSpec((None, 1, tf), lambda i, f: (layer, 0, f + nf)),
                  pl.BlockSpec((None, tf, D), lambda i, f: (layer, f, 0)),
                  pl.BlockSpec((1, D), lambda i, f: (0, 0))],
        out_specs=pl.BlockSpec((tm, D), lambda i, f: (i, 0)),
        out_shape=jax.ShapeDtypeStruct((T, D), F32),
        scratch_shapes=[pltpu.VMEM((tm, D), BF16),
                        pltpu.VMEM((tm, D), F32),
                        pltpu.VMEM((2, SUBLANES + tm, tf), F32),
                        pltpu.VMEM((nf, 2, SUBLANES, tf), F32)],
        compiler_params=_params(("arbitrary", "arbitrary"), 56),
        name="conv_ffn",
    )(x, gpre.reshape(1, D), w_up, w_up, conv_w, conv_w, conv_b, conv_b, w_down, gpost.reshape(1, D))


def _rope_head_permutation():
    half = ROT_DIM // 2
    mid = HEAD_DIM // 2
    return np.concatenate([np.arange(0, half), np.arange(ROT_DIM, mid + half),
                           np.arange(half, ROT_DIM), np.arange(mid + half, HEAD_DIM)])


def _rope_tables(positions):
    half = ROT_DIM // 2
    mid = HEAD_DIM // 2
    inv_freq = ROPE_THETA ** (-jnp.arange(0, ROT_DIM, 2, dtype=F32) / ROT_DIM)
    ang = positions.astype(F32).reshape(-1, 1) * inv_freq
    cos, sin = jnp.cos(ang), jnp.sin(ang)
    T = ang.shape[0]
    ones = jnp.ones((T, mid - half), F32)
    zeros = jnp.zeros((T, mid - half), F32)
    c = jnp.concatenate([cos, ones, cos, ones], axis=1)
    s = jnp.concatenate([-sin, zeros, sin, zeros], axis=1)
    return jnp.stack([c, jnp.ones_like(c)]), jnp.stack([s, jnp.zeros_like(s)])


def _scale_cast_body(w_ref, s_ref, o_ref):
    o_ref[...] = (w_ref[...] * s_ref[...]).astype(o_ref.dtype)


def scale_cast_columns(w, col_scale, *, tn):
    n_layers, D, _ = w.shape
    N = col_scale.shape[1]
    return pl.pallas_call(
        _scale_cast_body,
        grid=(n_layers, N // tn),
        in_specs=[pl.BlockSpec((None, D, tn), lambda l, j: (l, 0, j)),
                  pl.BlockSpec((1, tn), lambda l, j: (0, j))],
        out_specs=pl.BlockSpec((None, D, tn), lambda l, j: (l, 0, j)),
        out_shape=jax.ShapeDtypeStruct((n_layers, D, N), BF16),
        compiler_params=_params(("parallel", "parallel"), 32),
        name="scale_cast_columns",
    )(w, col_scale)


def _prepare_even_w_in(w_in_ab):
    n_layers = w_in_ab.shape[0]
    part = N_HEADS_A * HEAD_DIM
    scale = HEAD_DIM ** -0.5 * LOG2E
    col_scale = np.ones((6, part), np.float32)
    col_scale[[0, 3]] = scale
    wb = (w_in_ab * col_scale.reshape(1, 1, 6 * part)).astype(BF16)
    perm = _rope_head_permutation()
    pmat = np.zeros((HEAD_DIM, HEAD_DIM), np.float32)
    pmat[perm, np.arange(HEAD_DIM)] = 1.0

    def permute(cols):
        heads = cols.reshape(n_layers, D_MODEL, -1, HEAD_DIM)
        out = jnp.einsum('ldhe,ef->ldhf', heads, jnp.asarray(pmat, BF16), preferred_element_type=BF16)
        return out.reshape(cols.shape)

    return jnp.concatenate([permute(wb[..., :2 * part]), wb[..., 2 * part:3 * part],
                            permute(wb[..., 3 * part:5 * part]), wb[..., 5 * part:]], axis=-1)


def _even_mixer(x, g_pre, w_in, w_out, layer, g_post, rope, *, batch, seq):
    part = N_HEADS_A * HEAD_DIM
    qkv = norm_proj(x, g_pre, w_in, layer, tm=512, tn=part, out_dtype=BF16, rope=rope, rope_period=3)
    hb = N_HEADS_A
    oa = dilated_attention(qkv, batch=batch, seq=seq, n_heads=N_HEADS_A,
                           q_col=0, k_col=hb, v_col=2 * hb, tq=512)
    kbar = moba_block_means(qkv, batch=batch, seq=seq, n_heads=N_HEADS_B, k_col=4 * hb)
    ob = moba_attention(qkv, kbar, batch=batch, seq=seq, n_heads=N_HEADS_B,
                        q_col=3 * hb, k_col=4 * hb, v_col=5 * hb, tq=512)
    return proj_norm_res([oa, ob], w_out, layer, x, g_post, tm=512)


def _odd_mixer(x, g_pre, w_qkv, w_f, b_f, w_out, layer, g_post, *, batch, seq):
    qkv = norm_proj(x, g_pre, w_qkv, layer, tm=512, tn=1024, out_dtype=BF16)
    bf = jnp.pad(b_f.astype(F32), (0, LANES - N_HEADS)).reshape(1, LANES)
    c_col = fox_gate_cumsum(x, g_pre, w_f, layer, bf, batch=batch, seq=seq, tm=512)
    o = fox_attention(qkv, c_col, batch=batch, seq=seq, n_heads=N_HEADS, tq=512)
    return proj_norm_res([o], w_out, layer, x, g_post, tm=512)


def kernel(x, mem, positions, w_in_ab, w_out_ab, w_in_c, b_f, w_out_c, g_mix_pre, g_mix_post, g_mem_kv, g_mem_pre, g_mem_post, w_mq, w_mk, w_mv, w_mo, g_ffn_pre, g_ffn_post, w_up, conv_w, conv_b, w_down):
    batch, seq, d = x.shape
    mem_tokens = mem.shape[1]
    depth = g_mix_pre.shape[0]
    rope = _rope_tables(positions)
    xs = x.reshape(batch * seq, d)
    mems = mem.reshape(batch * mem_tokens, d)

    nqkv = 3 * D_MODEL
    q_scale = np.ones((1, nqkv), np.float32)
    q_scale[:, :D_MODEL] = HEAD_DIM ** -0.5 * LOG2E
    w_in_even = _prepare_even_w_in(w_in_ab)
    w_qkv_odd = scale_cast_columns(w_in_c, jnp.asarray(q_scale), tn=1024)
    w_f_odd = jnp.pad(w_in_c[..., nqkv:], ((0, 0), (0, 0), (0, LANES - N_HEADS))).astype(BF16)
    w_out_even, w_out_odd = w_out_ab.astype(BF16), w_out_c.astype(BF16)
    w_mem_q = (w_mq * MEM_HEAD_DIM ** -0.5).astype(BF16)
    w_mem_kv = jnp.concatenate([w_mk, w_mv], axis=2).astype(BF16)
    w_mem_o = w_mo.astype(BF16)
    w_up_b, w_down_b = w_up.astype(BF16), w_down.astype(BF16)
    conv_b3 = conv_b.reshape(depth, 1, conv_b.shape[1])

    for layer in range(depth):
        if layer % 2 == 0:
            xs = _even_mixer(xs, g_mix_pre[layer], w_in_even, w_out_even, layer // 2,
                             g_mix_post[layer], rope, batch=batch, seq=seq)
        else:
            xs = _odd_mixer(xs, g_mix_pre[layer], w_qkv_odd, w_f_odd, b_f[layer // 2], w_out_odd, layer // 2,
                            g_mix_post[layer], batch=batch, seq=seq)
        kv = norm_proj(mems, g_mem_kv[layer], w_mem_kv, layer, tm=512, tn=w_mem_kv.shape[2], out_dtype=BF16)
        xs = mem_attention(xs, g_mem_pre[layer], w_mem_q, kv, w_mem_o, layer, g_mem_post[layer],
                           seq=seq, mem_tokens=mem_tokens, tm=512)
        xs = conv_ffn(xs, g_ffn_pre[layer], w_up_b, conv_w, conv_b3, w_down_b, layer, g_ffn_post[layer],
                      seq=seq, tm=512, tf=512)
    return xs.reshape(batch, seq, d)
```

```python
import functools

import numpy as np
import jax
import jax.numpy as jnp
from jax import lax
from jax.experimental import pallas as pl
from jax.experimental.pallas import tpu as pltpu

D_MODEL = 2048
HEAD_DIM = 128
N_HEADS = D_MODEL // HEAD_DIM
N_HEADS_A = N_HEADS // 2
N_HEADS_B = N_HEADS - N_HEADS_A
DILATED_CONFIGS = ((128, 1), (512, 4), (2048, 16))
MOBA_BLOCK = 256
MOBA_TOPK = 3
ROT_DIM = HEAD_DIM // 4
ROPE_THETA = 500000.0
MEM_HEADS = 4
MEM_HEAD_DIM = 128
D_FF = ((8 * D_MODEL // 3 + 255) // 256) * 256
CONV_WIDTH = 3
RMS_EPS = 1e-6

LANES = 128
SUBLANES = 8
NEG = -1e30
LOG2E = 1.4426950408889634
HEADS_PER_STEP = 4
F32 = jnp.float32
BF16 = jnp.bfloat16
MIB = 1024 * 1024


def _params(semantics, vmem_mib):
    return pltpu.CompilerParams(dimension_semantics=semantics,
                                vmem_limit_bytes=vmem_mib * MIB)


def _rms(x, g):
    ms = jnp.mean(x * x, axis=-1, keepdims=True)
    return x * lax.rsqrt(ms + RMS_EPS) * g


def _dot(a, b):
    return jnp.dot(a, b, preferred_element_type=F32)


def _dot_nt(a, b):
    return lax.dot_general(a, b, (((1,), (1,)), ((), ())), preferred_element_type=F32)


def _softmax_init(m_refs, acc_refs):
    for m_ref, acc_ref in zip(m_refs, acc_refs):
        m_ref[...] = jnp.full(m_ref.shape, NEG, F32)
        acc_ref[...] = jnp.zeros(acc_ref.shape, F32)


def _multi_softmax_step(s_list, v_list, m_refs, acc_refs):
    m_prev = [r[...] for r in m_refs]
    m_new = [jnp.maximum(mp, jnp.max(s, axis=1, keepdims=True)) for mp, s in zip(m_prev, s_list)]
    p_list = [jnp.exp2(s - _tile_lanes(mn, s.shape[1])) for s, mn in zip(s_list, m_new)]
    for a, p in enumerate(p_list):
        alpha = jnp.exp2(m_prev[a] - m_new[a])
        acc_ref = acc_refs[a]
        acc_ref[...] = _tile_lanes(alpha, acc_ref.shape[1]) * acc_ref[...] + _dot(p.astype(BF16), v_list[a])
        m_refs[a][...] = m_new[a]


def _softmax_finish(o_ref, acc_refs):
    for a, acc_ref in enumerate(acc_refs):
        acc = acc_ref[...]
        o_ref[:, _head_cols(a)] = (acc[:, :HEAD_DIM] / acc[:, HEAD_DIM:]).astype(o_ref.dtype)


def _tile_lanes(x, width):
    n = width // LANES
    return x if n == 1 else jnp.concatenate([x] * n, axis=1)


def _softmax_scratch(tq, nh):
    return [pltpu.VMEM((tq, LANES), F32)] * nh + [pltpu.VMEM((tq, 2 * HEAD_DIM), F32)] * nh


def _head_cols(a):
    return slice(a * HEAD_DIM, (a + 1) * HEAD_DIM)


def _fill_value_scratch(va_ref, v_ref):
    for a in range(va_ref.shape[0]):
        va_ref[a, :, :HEAD_DIM] = v_ref[:, _head_cols(a)]
        va_ref[a, :, HEAD_DIM:] = jnp.ones((va_ref.shape[1], HEAD_DIM), va_ref.dtype)


def _causal_flash(i, qa_ref, ka_ref, va_ref, m_refs, acc_refs, o_ref):
    nh, tq, _ = qa_ref.shape
    _softmax_init(m_refs, acc_refs)

    def step(first_tile, n_tiles, diagonal):
        start = pl.multiple_of(first_tile * tq, tq)
        width = n_tiles * tq
        s_list = [_dot_nt(qa_ref[a], ka_ref[a, pl.ds(start, width), :]) for a in range(nh)]
        v_list = [va_ref[a, pl.ds(start, width), :] for a in range(nh)]
        if diagonal:
            row = lax.broadcasted_iota(jnp.int32, (tq, width), 0)
            col = lax.broadcasted_iota(jnp.int32, (tq, width), 1)
            s_list = [jnp.where(col - (width - tq) <= row, s, NEG) for s in s_list]
        _multi_softmax_step(s_list, v_list, m_refs, acc_refs)

    def below_diagonal_pair(jj, carry):
        step(2 * jj, 2, False)
        return carry

    lax.fori_loop(0, i // 2, below_diagonal_pair, 0)

    @pl.when(i % 2 == 1)
    def _():
        step(i - 1, 2, True)

    @pl.when(i % 2 == 0)
    def _():
        step(i, 1, True)

    _softmax_finish(o_ref, acc_refs)


def _norm_proj_body(x_ref, g_ref, w_ref, o_ref, h_ref):
    @pl.when(pl.program_id(1) == 0)
    def _():
        h_ref[...] = _rms(x_ref[...], g_ref[...]).astype(BF16)

    o_ref[...] = _dot(h_ref[...], w_ref[...]).astype(o_ref.dtype)


def _norm_proj_rope_body(x_ref, g_ref, w_ref, cos_ref, sin_ref, o_ref, h_ref):
    @pl.when(pl.program_id(1) == 0)
    def _():
        h_ref[...] = _rms(x_ref[...], g_ref[...]).astype(BF16)

    acc = _dot(h_ref[...], w_ref[...])
    c, s = cos_ref[0], sin_ref[0]
    for hh in range(acc.shape[1] // HEAD_DIM):
        t = acc[:, hh * HEAD_DIM:(hh + 1) * HEAD_DIM]
        r = t * c + pltpu.roll(t, HEAD_DIM // 2, 1) * s
        o_ref[:, hh * HEAD_DIM:(hh + 1) * HEAD_DIM] = r.astype(o_ref.dtype)


def norm_proj(x, g, w, layer, *, tm, tn, out_dtype, rope=None, rope_period=None):
    T, D = x.shape
    N = w.shape[2]
    grid = (T // tm, N // tn)
    in_specs = [pl.BlockSpec((tm, D), lambda i, j: (i, 0)),
                pl.BlockSpec((1, D), lambda i, j: (0, 0)),
                pl.BlockSpec((None, D, tn), lambda i, j: (layer, 0, j))]
    args = [x, g.reshape(1, D), w]
    if rope is None:
        body = _norm_proj_body
    else:
        body = _norm_proj_rope_body
        table_spec = pl.BlockSpec((1, tm, HEAD_DIM),
                                  lambda i, j: ((j % rope_period) // (rope_period - 1), i, 0))
        in_specs += [table_spec] * 2
        args += list(rope)
    return pl.pallas_call(
        body,
        grid=grid,
        in_specs=in_specs,
        out_specs=pl.BlockSpec((tm, tn), lambda i, j: (i, j)),
        out_shape=jax.ShapeDtypeStruct((T, N), out_dtype),
        scratch_shapes=[pltpu.VMEM((tm, D), BF16)],
        compiler_params=_params(("parallel", "arbitrary"), 48),
        name="norm_proj",
    )(*args)


def _proj_norm_res_body(*refs):
    *a_refs, w_ref, x_ref, g_ref, o_ref = refs
    y, row = None, 0
    for a_ref in a_refs:
        k = a_ref.shape[1]
        part = _dot(a_ref[...], w_ref[row:row + k, :])
        y = part if y is None else y + part
        row += k
    o_ref[...] = x_ref[...] + _rms(y, g_ref[...])


def proj_norm_res(a_parts, w, layer, x, g, *, tm):
    T = x.shape[0]
    _, K, D = w.shape
    return pl.pallas_call(
        _proj_norm_res_body,
        grid=(T // tm,),
        in_specs=[pl.BlockSpec((tm, a.shape[1]), lambda i: (i, 0)) for a in a_parts]
                 + [pl.BlockSpec((None, K, D), lambda i: (layer, 0, 0)),
                    pl.BlockSpec((tm, D), lambda i: (i, 0)),
                    pl.BlockSpec((1, D), lambda i: (0, 0))],
        out_specs=pl.BlockSpec((tm, D), lambda i: (i, 0)),
        out_shape=jax.ShapeDtypeStruct((T, D), F32),
        compiler_params=_params(("parallel",), 48),
        name="proj_norm_res",
    )(*a_parts, w, x, g.reshape(1, D))


def _dilated_log_multiplicity(tq, n_back):
    r = np.arange(tq)[:, None]
    c = np.arange(tq)[None, :]
    out = []
    for j in range(n_back + 1):
        d = (n_back - j) * tq + r - c
        mult = np.zeros((tq, tq), np.float64)
        for window, dil in DILATED_CONFIGS:
            mult += (d >= 0) & (d % dil == 0) & (d <= window)
        out.append(np.where(mult > 0, np.log2(np.maximum(mult, 1.0)), NEG))
    return np.stack(out).astype(np.float32)


def _dilated_body(q_ref, k_ref, v_ref, lb_ref, o_ref, va_ref, *scratch, n_back):
    nh = HEADS_PER_STEP
    m_refs, acc_refs = scratch[:nh], scratch[nh:]
    i = pl.program_id(2)
    tq = q_ref.shape[0]

    @pl.when(i == 0)
    def _():
        _fill_value_scratch(va_ref, v_ref)

    _softmax_init(m_refs, acc_refs)

    def step(first_j, n_tiles):
        start = pl.multiple_of((i - n_back + first_j) * tq, tq)
        width = n_tiles * tq
        lb = [lb_ref[first_j + t] for t in range(n_tiles)]
        lb = lb[0] if n_tiles == 1 else jnp.concatenate(lb, axis=1)
        s_list = [_dot_nt(q_ref[:, _head_cols(a)], k_ref[pl.ds(start, width), _head_cols(a)]) + lb
                  for a in range(nh)]
        v_list = [va_ref[a, pl.ds(start, width), :] for a in range(nh)]
        _multi_softmax_step(s_list, v_list, m_refs, acc_refs)

    total = jnp.minimum(i, n_back) + 1
    j_lo = n_back + 1 - total

    @pl.when(total % 2 == 1)
    def _():
        step(j_lo, 1)

    def pair(p, carry):
        step(j_lo + total % 2 + 2 * p, 2)
        return carry

    lax.fori_loop(0, total // 2, pair, 0)
    _softmax_finish(o_ref, acc_refs)


def dilated_attention(qkv, *, batch, seq, n_heads, q_col, k_col, v_col, tq):
    nq = seq // tq
    nh = HEADS_PER_STEP
    groups = n_heads // nh
    width = nh * HEAD_DIM
    max_window = max(w for w, _ in DILATED_CONFIGS)
    n_back = -(-max_window // tq)
    lb = jnp.asarray(_dilated_log_multiplicity(tq, n_back))
    qg, kg, vg = q_col // nh, k_col // nh, v_col // nh
    return pl.pallas_call(
        functools.partial(_dilated_body, n_back=n_back),
        grid=(batch, groups, nq),
        in_specs=[pl.BlockSpec((tq, width), lambda b, g, i: (b * nq + i, qg + g)),
                  pl.BlockSpec((seq, width), lambda b, g, i: (b, kg + g)),
                  pl.BlockSpec((seq, width), lambda b, g, i: (b, vg + g)),
                  pl.BlockSpec((n_back + 1, tq, tq), lambda b, g, i: (0, 0, 0))],
        out_specs=pl.BlockSpec((tq, width), lambda b, g, i: (b * nq + i, g)),
        out_shape=jax.ShapeDtypeStruct((batch * seq, n_heads * HEAD_DIM), BF16),
        scratch_shapes=[pltpu.VMEM((nh, seq, 2 * HEAD_DIM), BF16)] + _softmax_scratch(tq, nh),
        compiler_params=_params(("parallel", "parallel", "arbitrary"), 58),
        name="dilated_attention",
    )(qkv, qkv, qkv, lb)


def _block_mean_body(k_ref, o_ref):
    n = pl.program_id(1)
    mean = jnp.mean(k_ref[...].astype(F32), axis=0, keepdims=True)
    o_ref[0, pl.ds(n, 1), :] = mean


def moba_block_means(qkv, *, batch, seq, n_heads, k_col):
    nblk = seq // MOBA_BLOCK
    width = n_heads * HEAD_DIM
    kc = k_col * HEAD_DIM // width
    return pl.pallas_call(
        _block_mean_body,
        grid=(batch, nblk),
        in_specs=[pl.BlockSpec((MOBA_BLOCK, width), lambda b, n: (b * nblk + n, kc))],
        out_specs=pl.BlockSpec((1, nblk, width), lambda b, n: (b, 0, 0)),
        out_shape=jax.ShapeDtypeStruct((batch, nblk, width), F32),
        compiler_params=_params(("parallel", "arbitrary"), 16),
        name="moba_block_means",
    )(qkv)


def _moba_select_bias(q, kbar, first_row, nblk):
    rows = q.shape[0]
    kb_hi = kbar.astype(BF16)
    kb_lo = (kbar - kb_hi.astype(F32)).astype(BF16)
    gate = _dot_nt(kb_hi, q) + _dot_nt(kb_lo, q)
    blk = lax.broadcasted_iota(jnp.int32, gate.shape, 0)
    blk_f = blk.astype(F32)
    own = (first_row + lax.broadcasted_iota(jnp.int32, gate.shape, 1)) // MOBA_BLOCK
    past = blk < own
    g = jnp.where(past, gate, -jnp.inf)
    selected = jnp.zeros(gate.shape, jnp.bool_)
    for _ in range(MOBA_TOPK):
        mx = jnp.max(g, axis=0, keepdims=True)
        first = jnp.min(jnp.where(g == mx, blk_f, float(nblk)), axis=0, keepdims=True)
        pick = (blk_f == first) & past
        selected = selected | pick
        g = jnp.where(pick, -jnp.inf, g)
    sel = jnp.where(selected | (blk == own), 0.0, NEG)
    sel = jnp.concatenate([sel, jnp.zeros((LANES - nblk, rows), F32)], axis=0)
    return sel.T


def _moba_body(q_ref, k_ref, v_ref, kbar_ref, onehot_ref, o_ref, qa_ref, ka_ref, va_ref, *scratch, nblk):
    nh = HEADS_PER_STEP
    m_refs, acc_refs = scratch[:nh], scratch[nh:]
    i = pl.program_id(2)
    tq = q_ref.shape[0]

    @pl.when(i == 0)
    def _():
        _fill_value_scratch(va_ref, v_ref)
        for a in range(nh):
            ka_ref[a, :, :HEAD_DIM] = k_ref[:, _head_cols(a)]
            ka_ref[a, :, HEAD_DIM:] = onehot_ref[...]

    for a in range(nh):
        q = q_ref[:, _head_cols(a)]
        qa_ref[a, :, :HEAD_DIM] = q
        qa_ref[a, :, HEAD_DIM:] = _moba_select_bias(q, kbar_ref[0, :, _head_cols(a)], i * tq, nblk).astype(BF16)

    _causal_flash(i, qa_ref, ka_ref, va_ref, m_refs, acc_refs, o_ref)


def moba_attention(qkv, kbar, *, batch, seq, n_heads, q_col, k_col, v_col, tq):
    nblk = seq // MOBA_BLOCK
    nq = seq // tq
    nh = HEADS_PER_STEP
    groups = n_heads // nh
    width = nh * HEAD_DIM
    qg, kg, vg = q_col // nh, k_col // nh, v_col // nh
    onehot = (np.arange(seq)[:, None] // MOBA_BLOCK == np.arange(LANES)[None, :]).astype(np.float32)
    return pl.pallas_call(
        functools.partial(_moba_body, nblk=nblk),
        grid=(batch, groups, nq),
        in_specs=[pl.BlockSpec((tq, width), lambda b, g, i: (b * nq + i, qg + g)),
                  pl.BlockSpec((seq, width), lambda b, g, i: (b, kg + g)),
                  pl.BlockSpec((seq, width), lambda b, g, i: (b, vg + g)),
                  pl.BlockSpec((1, nblk, width), lambda b, g, i: (b, 0, g)),
                  pl.BlockSpec((seq, LANES), lambda b, g, i: (0, 0))],
        out_specs=pl.BlockSpec((tq, width), lambda b, g, i: (b * nq + i, g)),
        out_shape=jax.ShapeDtypeStruct((batch * seq, n_heads * HEAD_DIM), BF16),
        scratch_shapes=[pltpu.VMEM((nh, tq, 2 * HEAD_DIM), BF16),
                        pltpu.VMEM((nh, seq, 2 * HEAD_DIM), BF16),
                        pltpu.VMEM((nh, seq, 2 * HEAD_DIM), BF16)] + _softmax_scratch(tq, nh),
        compiler_params=_params(("parallel", "parallel", "arbitrary"), 58),
        name="moba_attention",
    )(qkv, qkv, qkv, kbar, jnp.asarray(onehot, BF16))


def _fox_gate_body(x_ref, g_ref, wf_ref, bf_ref, tri_ref, c_ref, carry_ref):
    @pl.when(pl.program_id(1) == 0)
    def _():
        carry_ref[...] = jnp.zeros(carry_ref.shape, F32)

    h = _rms(x_ref[...], g_ref[...]).astype(BF16)
    fg = _dot(h, wf_ref[...]) + bf_ref[...]
    logf = (jnp.minimum(fg, 0.0) - jnp.log(1.0 + jnp.exp(-jnp.abs(fg)))) * LOG2E
    hi = logf.astype(BF16)
    rem = logf - hi.astype(F32)
    mid = rem.astype(BF16)
    lo = (rem - mid.astype(F32)).astype(BF16)
    tri = tri_ref[...]
    c = _dot(tri, hi) + _dot(tri, mid) + _dot(tri, lo) + carry_ref[0:1, :]
    c_ref[...] = c
    carry_ref[...] = jnp.broadcast_to(c[c.shape[0] - 1:, :], carry_ref.shape)


def fox_gate_cumsum(x, g, wf, layer, bf, *, batch, seq, tm):
    T, D = x.shape
    ns = seq // tm
    tri = jnp.asarray(np.tril(np.ones((tm, tm), np.float32)), BF16)
    return pl.pallas_call(
        _fox_gate_body,
        grid=(batch, ns),
        in_specs=[pl.BlockSpec((tm, D), lambda b, s: (b * ns + s, 0)),
                  pl.BlockSpec((1, D), lambda b, s: (0, 0)),
                  pl.BlockSpec((None, D, LANES), lambda b, s: (layer, 0, 0)),
                  pl.BlockSpec((1, LANES), lambda b, s: (0, 0)),
                  pl.BlockSpec((tm, tm), lambda b, s: (0, 0))],
        out_specs=pl.BlockSpec((tm, LANES), lambda b, s: (b * ns + s, 0)),
        out_shape=jax.ShapeDtypeStruct((T, LANES), F32),
        scratch_shapes=[pltpu.VMEM((SUBLANES, LANES), F32)],
        compiler_params=_params(("parallel", "arbitrary"), 32),
        name="fox_gate_cumsum",
    )(x, g.reshape(1, D), wf, bf, tri)


def _split3(c):
    hi = c.astype(BF16).astype(F32)
    mid = (c - hi).astype(BF16).astype(F32)
    lo = (c - hi - mid).astype(BF16).astype(F32)
    return hi, mid, lo


def _fox_bias_factors(cc, head, negate):
    lane = lax.broadcasted_iota(jnp.int32, cc.shape, 1)
    c = jnp.sum(jnp.where(lane == head, cc, 0.0), axis=1, keepdims=True)
    hi, mid, lo = _split3(-c if negate else c)
    base = 3 if negate else 0
    terms = jnp.where(lane == base, hi, jnp.where(lane == base + 1, mid, lo))
    is_term = (lane >= base) & (lane < base + 3)
    return jnp.where(is_term, terms, jnp.where(lane < 6, 1.0, 0.0))


def _fox_body(q_ref, k_ref, v_ref, c_ref, o_ref, qa_ref, ka_ref, va_ref, *scratch):
    nh = HEADS_PER_STEP
    m_refs, acc_refs = scratch[:nh], scratch[nh:]
    g = pl.program_id(1)
    i = pl.program_id(2)
    tq = q_ref.shape[0]

    @pl.when(i == 0)
    def _():
        _fill_value_scratch(va_ref, v_ref)
        cc = c_ref[...]
        for a in range(nh):
            ka_ref[a, :, :HEAD_DIM] = k_ref[:, _head_cols(a)]
            ka_ref[a, :, HEAD_DIM:] = _fox_bias_factors(cc, g * nh + a, True).astype(BF16)

    cc = c_ref[pl.ds(pl.multiple_of(i * tq, tq), tq), :]
    for a in range(nh):
        qa_ref[a, :, :HEAD_DIM] = q_ref[:, _head_cols(a)]
        qa_ref[a, :, HEAD_DIM:] = _fox_bias_factors(cc, g * nh + a, False).astype(BF16)

    _causal_flash(i, qa_ref, ka_ref, va_ref, m_refs, acc_refs, o_ref)


def fox_attention(qkv, c_col, *, batch, seq, n_heads, tq):
    nq = seq // tq
    nh = HEADS_PER_STEP
    groups = n_heads // nh
    width = nh * HEAD_DIM
    return pl.pallas_call(
        _fox_body,
        grid=(batch, groups, nq),
        in_specs=[pl.BlockSpec((tq, width), lambda b, g, i: (b * nq + i, g)),
                  pl.BlockSpec((seq, width), lambda b, g, i: (b, groups + g)),
                  pl.BlockSpec((seq, width), lambda b, g, i: (b, 2 * groups + g)),
                  pl.BlockSpec((seq, LANES), lambda b, g, i: (b, 0))],
        out_specs=pl.BlockSpec((tq, width), lambda b, g, i: (b * nq + i, g)),
        out_shape=jax.ShapeDtypeStruct((batch * seq, n_heads * HEAD_DIM), BF16),
        scratch_shapes=[pltpu.VMEM((nh, tq, 2 * HEAD_DIM), BF16),
                        pltpu.VMEM((nh, seq, 2 * HEAD_DIM), BF16),
                        pltpu.VMEM((nh, seq, 2 * HEAD_DIM), BF16)] + _softmax_scratch(tq, nh),
        compiler_params=_params(("parallel", "parallel", "arbitrary"), 58),
        name="fox_attention",
    )(qkv, qkv, qkv, c_col)


def _mem_attn_body(x_ref, gpre_ref, wq_ref, kv_ref, wo_ref, gpost_ref, o_ref):
    x = x_ref[...]
    h = _rms(x, gpre_ref[...]).astype(BF16)
    q = _dot(h, wq_ref[...]).astype(BF16)
    width = MEM_HEADS * MEM_HEAD_DIM
    outs = []
    for hd in range(MEM_HEADS):
        lo, hi = hd * MEM_HEAD_DIM, (hd + 1) * MEM_HEAD_DIM
        s = _dot_nt(q[:, lo:hi], kv_ref[:, lo:hi])
        p = jnp.exp(s - jnp.max(s, axis=1, keepdims=True))
        o = _dot(p.astype(BF16), kv_ref[:, width + lo:width + hi])
        outs.append((o / jnp.sum(p, axis=1, keepdims=True)).astype(BF16))
    y = _dot(jnp.concatenate(outs, axis=1), wo_ref[...])
    o_ref[...] = x + _rms(y, gpost_ref[...])


def mem_attention(x, gpre, wq, kv, wo, layer, gpost, *, seq, mem_tokens, tm):
    T, D = x.shape
    width = wq.shape[2]
    per_seq = seq // tm
    return pl.pallas_call(
        _mem_attn_body,
        grid=(T // tm,),
        in_specs=[pl.BlockSpec((tm, D), lambda i: (i, 0)),
                  pl.BlockSpec((1, D), lambda i: (0, 0)),
                  pl.BlockSpec((None, D, width), lambda i: (layer, 0, 0)),
                  pl.BlockSpec((mem_tokens, 2 * width), lambda i: (i // per_seq, 0)),
                  pl.BlockSpec((None, width, D), lambda i: (layer, 0, 0)),
                  pl.BlockSpec((1, D), lambda i: (0, 0))],
        out_specs=pl.BlockSpec((tm, D), lambda i: (i, 0)),
        out_shape=jax.ShapeDtypeStruct((T, D), F32),
        compiler_params=_params(("parallel",), 48),
        name="mem_attention",
    )(x, gpre.reshape(1, D), wq, kv, wo, gpost.reshape(1, D))


def _ffn_body(x_ref, gpre_ref, wg_ref, wv_ref, cwg_ref, cwv_ref, cbg_ref, cbv_ref, wd_ref, gpost_ref,
              o_ref, h_ref, acc_ref, ubuf_ref, carry_ref, *, tiles_per_seq):
    i = pl.program_id(0)
    f = pl.program_id(1)
    tm = x_ref.shape[0]
    halo = SUBLANES

    @pl.when(f == 0)
    def _():
        h_ref[...] = _rms(x_ref[...], gpre_ref[...]).astype(BF16)
        acc_ref[...] = jnp.zeros(acc_ref.shape, F32)

    h = h_ref[...]
    seq_start = i % tiles_per_seq == 0

    def conv_branch(w_ref, cw_ref, cb_ref, slot):
        u = _dot(h, w_ref[...])
        ubuf_ref[slot, 0:halo, :] = jnp.where(seq_start, 0.0, carry_ref[f, slot])
        ubuf_ref[slot, halo:, :] = u
        carry_ref[f, slot] = u[tm - halo:, :]
        cw = cw_ref[...]
        return (cb_ref[...] + cw[0:1] * ubuf_ref[slot, halo - 2:halo - 2 + tm, :]
                + cw[1:2] * ubuf_ref[slot, halo - 1:halo - 1 + tm, :] + cw[2:3] * u)

    g = conv_branch(wg_ref, cwg_ref, cbg_ref, 0)
    v = conv_branch(wv_ref, cwv_ref, cbv_ref, 1)
    a = (g / (1.0 + jnp.exp(-g)) * v).astype(BF16)
    acc_ref[...] += _dot(a, wd_ref[...])

    @pl.when(f == pl.num_programs(1) - 1)
    def _():
        o_ref[...] = x_ref[...] + _rms(acc_ref[...], gpost_ref[...])


def conv_ffn(x, gpre, w_up, conv_w, conv_b, w_down, layer, gpost, *, seq, tm, tf):
    T, D = x.shape
    dff = w_down.shape[1]
    nf = dff // tf
    return pl.pallas_call(
        functools.partial(_ffn_body, tiles_per_seq=seq // tm),
        grid=(T // tm, nf),
        in_specs=[pl.BlockSpec((tm, D), lambda i, f: (i, 0)),
                  pl.BlockSpec((1, D), lambda i, f: (0, 0)),
                  pl.BlockSpec((None, D, tf), lambda i, f: (layer, 0, f)),
                  pl.BlockSpec((None, D, tf), lambda i, f: (layer, 0, f + nf)),
                  pl.BlockSpec((None, CONV_WIDTH, tf), lambda i, f: (layer, 0, f)),
                  pl.BlockSpec((None, CONV_WIDTH, tf), lambda i, f: (layer, 0, f + nf)),
                  pl.BlockSpec((None, 1, tf), lambda i, f: (layer, 0, f)),
                  pl.BlockSpec((None, 1, tf), lambda i, f: (layer, 0, f + nf)),
                  pl.BlockSpec((None, tf, D), lambda i, f: (layer, f, 0)),
                  pl.BlockSpec((1, D), lambda i, f: (0, 0))],
        out_specs=pl.BlockSpec((tm, D), lambda i, f: (i, 0)),
        out_shape=jax.ShapeDtypeStruct((T, D), F32),
        scratch_shapes=[pltpu.VMEM((tm, D), BF16),
                        pltpu.VMEM((tm, D), F32),
                        pltpu.VMEM((2, SUBLANES + tm, tf), F32),
                        pltpu.VMEM((nf, 2, SUBLANES, tf), F32)],
        compiler_params=_params(("arbitrary", "arbitrary"), 56),
        name="conv_ffn",
    )(x, gpre.reshape(1, D), w_up, w_up, conv_w, conv_w, conv_b, conv_b, w_down, gpost.reshape(1, D))


def _rope_head_permutation():
    half = ROT_DIM // 2
    mid = HEAD_DIM // 2
    return np.concatenate([np.arange(0, half), np.arange(ROT_DIM, mid + half),
                           np.arange(half, ROT_DIM), np.arange(mid + half, HEAD_DIM)])


def _rope_tables(positions):
    half = ROT_DIM // 2
    mid = HEAD_DIM // 2
    inv_freq = ROPE_THETA ** (-jnp.arange(0, ROT_DIM, 2, dtype=F32) / ROT_DIM)
    ang = positions.astype(F32).reshape(-1, 1) * inv_freq
    cos, sin = jnp.cos(ang), jnp.sin(ang)
    T = ang.shape[0]
    ones = jnp.ones((T, mid - half), F32)
    zeros = jnp.zeros((T, mid - half), F32)
    c = jnp.concatenate([cos, ones, cos, ones], axis=1)
    s = jnp.concatenate([-sin, zeros, sin, zeros], axis=1)
    return jnp.stack([c, jnp.ones_like(c)]), jnp.stack([s, jnp.zeros_like(s)])


def _scale_cast_body(w_ref, s_ref, o_ref):
    o_ref[...] = (w_ref[...] * s_ref[...]).astype(o_ref.dtype)


def scale_cast_columns(w, col_scale, *, tn):
    n_layers, D, _ = w.shape
    N = col_scale.shape[1]
    return pl.pallas_call(
        _scale_cast_body,
        grid=(n_layers, N // tn),
        in_specs=[pl.BlockSpec((None, D, tn), lambda l, j: (l, 0, j)),
                  pl.BlockSpec((1, tn), lambda l, j: (0, j))],
        out_specs=pl.BlockSpec((None, D, tn), lambda l, j: (l, 0, j)),
        out_shape=jax.ShapeDtypeStruct((n_layers, D, N), BF16),
        compiler_params=_params(("parallel", "parallel"), 32),
        name="scale_cast_columns",
    )(w, col_scale)


def _prepare_even_w_in(w_in_ab):
    n_layers = w_in_ab.shape[0]
    part = N_HEADS_A * HEAD_DIM
    scale = HEAD_DIM ** -0.5 * LOG2E
    col_scale = np.ones((6, part), np.float32)
    col_scale[[0, 3]] = scale
    wb = (w_in_ab * col_scale.reshape(1, 1, 6 * part)).astype(BF16)
    perm = _rope_head_permutation()
    pmat = np.zeros((HEAD_DIM, HEAD_DIM), np.float32)
    pmat[perm, np.arange(HEAD_DIM)] = 1.0

    def permute(cols):
        heads = cols.reshape(n_layers, D_MODEL, -1, HEAD_DIM)
        out = jnp.einsum('ldhe,ef->ldhf', heads, jnp.asarray(pmat, BF16), preferred_element_type=BF16)
        return out.reshape(cols.shape)

    return jnp.concatenate([permute(wb[..., :2 * part]), wb[..., 2 * part:3 * part],
                            permute(wb[..., 3 * part:5 * part]), wb[..., 5 * part:]], axis=-1)


def _even_mixer(x, g_pre, w_in, w_out, layer, g_post, rope, *, batch, seq):
    part = N_HEADS_A * HEAD_DIM
    qkv = norm_proj(x, g_pre, w_in, layer, tm=512, tn=part, out_dtype=BF16, rope=rope, rope_period=3)
    hb = N_HEADS_A
    oa = dilated_attention(qkv, batch=batch, seq=seq, n_heads=N_HEADS_A,
                           q_col=0, k_col=hb, v_col=2 * hb, tq=512)
    kbar = moba_block_means(qkv, batch=batch, seq=seq, n_heads=N_HEADS_B, k_col=4 * hb)
    ob = moba_attention(qkv, kbar, batch=batch, seq=seq, n_heads=N_HEADS_B,
                        q_col=3 * hb, k_col=4 * hb, v_col=5 * hb, tq=512)
    return proj_norm_res([oa, ob], w_out, layer, x, g_post, tm=512)


def _odd_mixer(x, g_pre, w_qkv, w_f, b_f, w_out, layer, g_post, *, batch, seq):
    qkv = norm_proj(x, g_pre, w_qkv, layer, tm=512, tn=1024, out_dtype=BF16)
    bf = jnp.pad(b_f.astype(F32), (0, LANES - N_HEADS)).reshape(1, LANES)
    c_col = fox_gate_cumsum(x, g_pre, w_f, layer, bf, batch=batch, seq=seq, tm=512)
    o = fox_attention(qkv, c_col, batch=batch, seq=seq, n_heads=N_HEADS, tq=512)
    return proj_norm_res([o], w_out, layer, x, g_post, tm=512)


def kernel(x, mem, positions, w_in_ab, w_out_ab, w_in_c, b_f, w_out_c, g_mix_pre, g_mix_post, g_mem_kv, g_mem_pre, g_mem_post, w_mq, w_mk, w_mv, w_mo, g_ffn_pre, g_ffn_post, w_up, conv_w, conv_b, w_down):
    batch, seq, d = x.shape
    mem_tokens = mem.shape[1]
    depth = g_mix_pre.shape[0]
    rope = _rope_tables(positions)
    xs = x.reshape(batch * seq, d)
    mems = mem.reshape(batch * mem_tokens, d)

    nqkv = 3 * D_MODEL
    q_scale = np.ones((1, nqkv), np.float32)
    q_scale[:, :D_MODEL] = HEAD_DIM ** -0.5 * LOG2E
    w_in_even = _prepare_even_w_in(w_in_ab)
    w_qkv_odd = scale_cast_columns(w_in_c, jnp.asarray(q_scale), tn=1024)
    w_f_odd = jnp.pad(w_in_c[..., nqkv:], ((0, 0), (0, 0), (0, LANES - N_HEADS))).astype(BF16)
    w_out_even, w_out_odd = w_out_ab.astype(BF16), w_out_c.astype(BF16)
    w_mem_q = (w_mq * MEM_HEAD_DIM ** -0.5).astype(BF16)
    w_mem_kv = jnp.concatenate([w_mk, w_mv], axis=2).astype(BF16)
    w_mem_o = w_mo.astype(BF16)
    w_up_b, w_down_b = w_up.astype(BF16), w_down.astype(BF16)
    conv_b3 = conv_b.reshape(depth, 1, conv_b.shape[1])

    for layer in range(depth):
        if layer % 2 == 0:
            xs = _even_mixer(xs, g_mix_pre[layer], w_in_even, w_out_even, layer // 2,
                             g_mix_post[layer], rope, batch=batch, seq=seq)
        else:
            xs = _odd_mixer(xs, g_mix_pre[layer], w_qkv_odd, w_f_odd, b_f[layer // 2], w_out_odd, layer // 2,
                            g_mix_post[layer], batch=batch, seq=seq)
        kv = norm_proj(mems, g_mem_kv[layer], w_mem_kv, layer, tm=512, tn=w_mem_kv.shape[2], out_dtype=BF16)
        xs = mem_attention(xs, g_mem_pre[layer], w_mem_q, kv, w_mem_o, layer, g_mem_post[layer],
                           seq=seq, mem_tokens=mem_tokens, tm=512)
        xs = conv_ffn(xs, g_ffn_pre[layer], w_up_b, conv_w, conv_b3, w_down_b, layer, g_ffn_post[layer],
                      seq=seq, tm=512, tf=512)
    return xs.reshape(batch, seq, d)
```

```python
import functools

import numpy as np
import jax
import jax.numpy as jnp
from jax import lax
from jax.experimental import pallas as pl
from jax.experimental.pallas import tpu as pltpu

D_MODEL = 2048
HEAD_DIM = 128
N_HEADS = D_MODEL // HEAD_DIM
N_HEADS_A = N_HEADS // 2
N_HEADS_B = N_HEADS - N_HEADS_A
DILATED_CONFIGS = ((128, 1), (512, 4), (2048, 16))
MOBA_BLOCK = 256
MOBA_TOPK = 3
ROT_DIM = HEAD_DIM // 4
ROPE_THETA = 500000.0
MEM_HEADS = 4
MEM_HEAD_DIM = 128
D_FF = ((8 * D_MODEL // 3 + 255) // 256) * 256
CONV_WIDTH = 3
RMS_EPS = 1e-6

LANES = 128
SUBLANES = 8
NEG = -1e30
LOG2E = 1.4426950408889634
HEADS_PER_STEP = 4
F32 = jnp.float32
BF16 = jnp.bfloat16
MIB = 1024 * 1024


def _params(semantics, vmem_mib):
    return pltpu.CompilerParams(dimension_semantics=semantics,
                                vmem_limit_bytes=vmem_mib * MIB)


def _rms(x, g):
    ms = jnp.mean(x * x, axis=-1, keepdims=True)
    return x * lax.rsqrt(ms + RMS_EPS) * g


def _dot(a, b):
    return jnp.dot(a, b, preferred_element_type=F32)


def _dot_nt(a, b):
    return lax.dot_general(a, b, (((1,), (1,)), ((), ())), preferred_element_type=F32)


def _softmax_init(m_refs, acc_refs):
    for m_ref, acc_ref in zip(m_refs, acc_refs):
        m_ref[...] = jnp.full(m_ref.shape, NEG, F32)
        acc_ref[...] = jnp.zeros(acc_ref.shape, F32)


def _multi_softmax_step(s_list, v_list, m_refs, acc_refs):
    m_prev = [r[...] for r in m_refs]
    m_new = [jnp.maximum(mp, jnp.max(s, axis=1, keepdims=True)) for mp, s in zip(m_prev, s_list)]
    p_list = [jnp.exp2(s - _tile_lanes(mn, s.shape[1])) for s, mn in zip(s_list, m_new)]
    for a, p in enumerate(p_list):
        alpha = jnp.exp2(m_prev[a] - m_new[a])
        acc_ref = acc_refs[a]
        acc_ref[...] = _tile_lanes(alpha, acc_ref.shape[1]) * acc_ref[...] + _dot(p.astype(BF16), v_list[a])
        m_refs[a][...] = m_new[a]


def _softmax_finish(o_ref, acc_refs):
    for a, acc_ref in enumerate(acc_refs):
        acc = acc_ref[...]
        o_ref[:, _head_cols(a)] = (acc[:, :HEAD_DIM] / acc[:, HEAD_DIM:]).astype(o_ref.dtype)


def _tile_lanes(x, width):
    n = width // LANES
    return x if n == 1 else jnp.concatenate([x] * n, axis=1)


def _softmax_scratch(tq, nh):
    return [pltpu.VMEM((tq, LANES), F32)] * nh + [pltpu.VMEM((tq, 2 * HEAD_DIM), F32)] * nh


def _head_cols(a):
    return slice(a * HEAD_DIM, (a + 1) * HEAD_DIM)


def _fill_value_scratch(va_ref, v_ref):
    for a in range(va_ref.shape[0]):
        va_ref[a, :, :HEAD_DIM] = v_ref[:, _head_cols(a)]
        va_ref[a, :, HEAD_DIM:] = jnp.ones((va_ref.shape[1], HEAD_DIM), va_ref.dtype)


def _causal_flash(i, qa_ref, ka_ref, va_ref, m_refs, acc_refs, o_ref):
    nh, tq, _ = qa_ref.shape
    _softmax_init(m_refs, acc_refs)

    def step(first_tile, n_tiles, diagonal):
        start = pl.multiple_of(first_tile * tq, tq)
        width = n_tiles * tq
        s_list = [_dot_nt(qa_ref[a], ka_ref[a, pl.ds(start, width), :]) for a in range(nh)]
        v_list = [va_ref[a, pl.ds(start, width), :] for a in range(nh)]
        if diagonal:
            row = lax.broadcasted_iota(jnp.int32, (tq, width), 0)
            col = lax.broadcasted_iota(jnp.int32, (tq, width), 1)
            s_list = [jnp.where(col - (width - tq) <= row, s, NEG) for s in s_list]
        _multi_softmax_step(s_list, v_list, m_refs, acc_refs)

    def below_diagonal_pair(jj, carry):
        step(2 * jj, 2, False)
        return carry

    lax.fori_loop(0, i // 2, below_diagonal_pair, 0)

    @pl.when(i % 2 == 1)
    def _():
        step(i - 1, 2, True)

    @pl.when(i % 2 == 0)
    def _():
        step(i, 1, True)

    _softmax_finish(o_ref, acc_refs)


def _norm_proj_body(x_ref, g_ref, w_ref, o_ref, h_ref):
    @pl.when(pl.program_id(1) == 0)
    def _():
        h_ref[...] = _rms(x_ref[...], g_ref[...]).astype(BF16)

    o_ref[...] = _dot(h_ref[...], w_ref[...]).astype(o_ref.dtype)


def _norm_proj_rope_body(x_ref, g_ref, w_ref, cos_ref, sin_ref, o_ref, h_ref, *, rope_period):
    j = pl.program_id(1)

    @pl.when(j == 0)
    def _():
        h_ref[...] = _rms(x_ref[...], g_ref[...]).astype(BF16)

    acc = _dot(h_ref[...], w_ref[...])
    plain = j % rope_period == rope_period - 1
    c = jnp.where(plain, 1.0, cos_ref[...])
    s = jnp.where(plain, 0.0, sin_ref[...])
    for hh in range(acc.shape[1] // HEAD_DIM):
        t = acc[:, hh * HEAD_DIM:(hh + 1) * HEAD_DIM]
        r = t * c + pltpu.roll(t, HEAD_DIM // 2, 1) * s
        o_ref[:, hh * HEAD_DIM:(hh + 1) * HEAD_DIM] = r.astype(o_ref.dtype)


def norm_proj(x, g, w, layer, *, tm, tn, out_dtype, rope=None, rope_period=None):
    T, D = x.shape
    N = w.shape[2]
    grid = (T // tm, N // tn)
    in_specs = [pl.BlockSpec((tm, D), lambda i, j: (i, 0)),
                pl.BlockSpec((1, D), lambda i, j: (0, 0)),
                pl.BlockSpec((None, D, tn), lambda i, j: (layer, 0, j))]
    args = [x, g.reshape(1, D), w]
    if rope is None:
        body = _norm_proj_body
    else:
        body = functools.partial(_norm_proj_rope_body, rope_period=rope_period)
        in_specs += [pl.BlockSpec((tm, HEAD_DIM), lambda i, j: (i, 0))] * 2
        args += list(rope)
    return pl.pallas_call(
        body,
        grid=grid,
        in_specs=in_specs,
        out_specs=pl.BlockSpec((tm, tn), lambda i, j: (i, j)),
        out_shape=jax.ShapeDtypeStruct((T, N), out_dtype),
        scratch_shapes=[pltpu.VMEM((tm, D), BF16)],
        compiler_params=_params(("parallel", "arbitrary"), 48),
        name="norm_proj",
    )(*args)


def _proj_norm_res_body(*refs):
    *a_refs, w_ref, x_ref, g_ref, o_ref = refs
    y, row = None, 0
    for a_ref in a_refs:
        k = a_ref.shape[1]
        part = _dot(a_ref[...], w_ref[row:row + k, :])
        y = part if y is None else y + part
        row += k
    o_ref[...] = x_ref[...] + _rms(y, g_ref[...])


def proj_norm_res(a_parts, w, layer, x, g, *, tm):
    T = x.shape[0]
    _, K, D = w.shape
    return pl.pallas_call(
        _proj_norm_res_body,
        grid=(T // tm,),
        in_specs=[pl.BlockSpec((tm, a.shape[1]), lambda i: (i, 0)) for a in a_parts]
                 + [pl.BlockSpec((None, K, D), lambda i: (layer, 0, 0)),
                    pl.BlockSpec((tm, D), lambda i: (i, 0)),
                    pl.BlockSpec((1, D), lambda i: (0, 0))],
        out_specs=pl.BlockSpec((tm, D), lambda i: (i, 0)),
        out_shape=jax.ShapeDtypeStruct((T, D), F32),
        compiler_params=_params(("parallel",), 48),
        name="proj_norm_res",
    )(*a_parts, w, x, g.reshape(1, D))


def _dilated_log_multiplicity(tq, n_back):
    r = np.arange(tq)[:, None]
    c = np.arange(tq)[None, :]
    out = []
    for j in range(n_back + 1):
        d = (n_back - j) * tq + r - c
        mult = np.zeros((tq, tq), np.float64)
        for window, dil in DILATED_CONFIGS:
            mult += (d >= 0) & (d % dil == 0) & (d <= window)
        out.append(np.where(mult > 0, np.log2(np.maximum(mult, 1.0)), NEG))
    return np.stack(out).astype(np.float32)


def _dilated_body(q_ref, k_ref, v_ref, lb_ref, o_ref, va_ref, *scratch, n_back):
    nh = HEADS_PER_STEP
    m_refs, acc_refs = scratch[:nh], scratch[nh:]
    i = pl.program_id(2)
    tq = q_ref.shape[0]

    @pl.when(i == 0)
    def _():
        _fill_value_scratch(va_ref, v_ref)

    _softmax_init(m_refs, acc_refs)

    def step(first_j, n_tiles):
        start = pl.multiple_of((i - n_back + first_j) * tq, tq)
        width = n_tiles * tq
        lb = [lb_ref[first_j + t] for t in range(n_tiles)]
        lb = lb[0] if n_tiles == 1 else jnp.concatenate(lb, axis=1)
        s_list = [_dot_nt(q_ref[:, _head_cols(a)], k_ref[pl.ds(start, width), _head_cols(a)]) + lb
                  for a in range(nh)]
        v_list = [va_ref[a, pl.ds(start, width), :] for a in range(nh)]
        _multi_softmax_step(s_list, v_list, m_refs, acc_refs)

    total = jnp.minimum(i, n_back) + 1
    j_lo = n_back + 1 - total

    @pl.when(total % 2 == 1)
    def _():
        step(j_lo, 1)

    def pair(p, carry):
        step(j_lo + total % 2 + 2 * p, 2)
        return carry

    lax.fori_loop(0, total // 2, pair, 0)
    _softmax_finish(o_ref, acc_refs)


def dilated_attention(qkv, *, batch, seq, n_heads, q_col, k_col, v_col, tq):
    nq = seq // tq
    nh = HEADS_PER_STEP
    groups = n_heads // nh
    width = nh * HEAD_DIM
    max_window = max(w for w, _ in DILATED_CONFIGS)
    n_back = -(-max_window // tq)
    lb = jnp.asarray(_dilated_log_multiplicity(tq, n_back))
    qg, kg, vg = q_col // nh, k_col // nh, v_col // nh
    return pl.pallas_call(
        functools.partial(_dilated_body, n_back=n_back),
        grid=(batch, groups, nq),
        in_specs=[pl.BlockSpec((tq, width), lambda b, g, i: (b * nq + i, qg + g)),
                  pl.BlockSpec((seq, width), lambda b, g, i: (b, kg + g)),
                  pl.BlockSpec((seq, width), lambda b, g, i: (b, vg + g)),
                  pl.BlockSpec((n_back + 1, tq, tq), lambda b, g, i: (0, 0, 0))],
        out_specs=pl.BlockSpec((tq, width), lambda b, g, i: (b * nq + i, g)),
        out_shape=jax.ShapeDtypeStruct((batch * seq, n_heads * HEAD_DIM), BF16),
        scratch_shapes=[pltpu.VMEM((nh, seq, 2 * HEAD_DIM), BF16)] + _softmax_scratch(tq, nh),
        compiler_params=_params(("parallel", "parallel", "arbitrary"), 58),
        name="dilated_attention",
    )(qkv, qkv, qkv, lb)


def _moba_select_bias(q, kbar, first_row, nblk):
    rows = q.shape[0]
    kb_hi = kbar.astype(BF16)
    kb_lo = (kbar - kb_hi.astype(F32)).astype(BF16)
    gate = _dot_nt(kb_hi, q) + _dot_nt(kb_lo, q)
    blk = lax.broadcasted_iota(jnp.int32, gate.shape, 0)
    blk_f = blk.astype(F32)
    own = (first_row + lax.broadcasted_iota(jnp.int32, gate.shape, 1)) // MOBA_BLOCK
    past = blk < own
    g = jnp.where(past, gate, -jnp.inf)
    selected = jnp.zeros(gate.shape, jnp.bool_)
    for _ in range(MOBA_TOPK):
        mx = jnp.max(g, axis=0, keepdims=True)
        first = jnp.min(jnp.where(g == mx, blk_f, float(nblk)), axis=0, keepdims=True)
        pick = (blk_f == first) & past
        selected = selected | pick
        g = jnp.where(pick, -jnp.inf, g)
    sel = jnp.where(selected | (blk == own), 0.0, NEG)
    sel = jnp.concatenate([sel, jnp.zeros((LANES - nblk, rows), F32)], axis=0)
    return sel.T


def _moba_body(q_ref, k_ref, v_ref, onehot_ref, o_ref, qa_ref, ka_ref, va_ref, kbar_ref, *scratch, nblk):
    nh = HEADS_PER_STEP
    m_refs, acc_refs = scratch[:nh], scratch[nh:]
    i = pl.program_id(2)
    tq = q_ref.shape[0]

    @pl.when(i == 0)
    def _():
        _fill_value_scratch(va_ref, v_ref)
        for a in range(nh):
            ka_ref[a, :, :HEAD_DIM] = k_ref[:, _head_cols(a)]
            ka_ref[a, :, HEAD_DIM:] = onehot_ref[...]
            for n in range(nblk):
                k_blk = k_ref[n * MOBA_BLOCK:(n + 1) * MOBA_BLOCK, _head_cols(a)].astype(F32)
                kbar_ref[a, n:n + 1, :] = jnp.mean(k_blk, axis=0, keepdims=True)

    for a in range(nh):
        q = q_ref[:, _head_cols(a)]
        qa_ref[a, :, :HEAD_DIM] = q
        qa_ref[a, :, HEAD_DIM:] = _moba_select_bias(q, kbar_ref[a], i * tq, nblk).astype(BF16)

    _causal_flash(i, qa_ref, ka_ref, va_ref, m_refs, acc_refs, o_ref)


def moba_attention(qkv, *, batch, seq, n_heads, q_col, k_col, v_col, tq):
    nblk = seq // MOBA_BLOCK
    nq = seq // tq
    nh = HEADS_PER_STEP
    groups = n_heads // nh
    width = nh * HEAD_DIM
    qg, kg, vg = q_col // nh, k_col // nh, v_col // nh
    onehot = (np.arange(seq)[:, None] // MOBA_BLOCK == np.arange(LANES)[None, :]).astype(np.float32)
    return pl.pallas_call(
        functools.partial(_moba_body, nblk=nblk),
        grid=(batch, groups, nq),
        in_specs=[pl.BlockSpec((tq, width), lambda b, g, i: (b * nq + i, qg + g)),
                  pl.BlockSpec((seq, width), lambda b, g, i: (b, kg + g)),
                  pl.BlockSpec((seq, width), lambda b, g, i: (b, vg + g)),
                  pl.BlockSpec((seq, LANES), lambda b, g, i: (0, 0))],
        out_specs=pl.BlockSpec((tq, width), lambda b, g, i: (b * nq + i, g)),
        out_shape=jax.ShapeDtypeStruct((batch * seq, n_heads * HEAD_DIM), BF16),
        scratch_shapes=[pltpu.VMEM((nh, tq, 2 * HEAD_DIM), BF16),
                        pltpu.VMEM((nh, seq, 2 * HEAD_DIM), BF16),
                        pltpu.VMEM((nh, seq, 2 * HEAD_DIM), BF16),
                        pltpu.VMEM((nh, nblk, HEAD_DIM), F32)] + _softmax_scratch(tq, nh),
        compiler_params=_params(("parallel", "parallel", "arbitrary"), 58),
        name="moba_attention",
    )(qkv, qkv, qkv, jnp.asarray(onehot, BF16))


def _fox_gate_body(x_ref, g_ref, wf_ref, bf_ref, tri_ref, c_ref, carry_ref):
    @pl.when(pl.program_id(1) == 0)
    def _():
        carry_ref[...] = jnp.zeros(carry_ref.shape, F32)

    h = _rms(x_ref[...], g_ref[...]).astype(BF16)
    fg = _dot(h, wf_ref[...]) + bf_ref[...]
    logf = (jnp.minimum(fg, 0.0) - jnp.log(1.0 + jnp.exp(-jnp.abs(fg)))) * LOG2E
    hi = logf.astype(BF16)
    rem = logf - hi.astype(F32)
    mid = rem.astype(BF16)
    lo = (rem - mid.astype(F32)).astype(BF16)
    tri = tri_ref[...]
    c = _dot(tri, hi) + _dot(tri, mid) + _dot(tri, lo) + carry_ref[0:1, :]
    c_ref[...] = c
    carry_ref[...] = jnp.broadcast_to(c[c.shape[0] - 1:, :], carry_ref.shape)


def fox_gate_cumsum(x, g, wf, layer, bf, *, batch, seq, tm):
    T, D = x.shape
    ns = seq // tm
    tri = jnp.asarray(np.tril(np.ones((tm, tm), np.float32)), BF16)
    return pl.pallas_call(
        _fox_gate_body,
        grid=(batch, ns),
        in_specs=[pl.BlockSpec((tm, D), lambda b, s: (b * ns + s, 0)),
                  pl.BlockSpec((1, D), lambda b, s: (0, 0)),
                  pl.BlockSpec((None, D, LANES), lambda b, s: (layer, 0, 0)),
                  pl.BlockSpec((1, LANES), lambda b, s: (0, 0)),
                  pl.BlockSpec((tm, tm), lambda b, s: (0, 0))],
        out_specs=pl.BlockSpec((tm, LANES), lambda b, s: (b * ns + s, 0)),
        out_shape=jax.ShapeDtypeStruct((T, LANES), F32),
        scratch_shapes=[pltpu.VMEM((SUBLANES, LANES), F32)],
        compiler_params=_params(("parallel", "arbitrary"), 32),
        name="fox_gate_cumsum",
    )(x, g.reshape(1, D), wf, bf, tri)


def _split3(c):
    hi = c.astype(BF16).astype(F32)
    mid = (c - hi).astype(BF16).astype(F32)
    lo = (c - hi - mid).astype(BF16).astype(F32)
    return hi, mid, lo


def _fox_bias_factors(cc, head, negate):
    lane = lax.broadcasted_iota(jnp.int32, cc.shape, 1)
    c = jnp.sum(jnp.where(lane == head, cc, 0.0), axis=1, keepdims=True)
    hi, mid, lo = _split3(-c if negate else c)
    base = 3 if negate else 0
    terms = jnp.where(lane == base, hi, jnp.where(lane == base + 1, mid, lo))
    is_term = (lane >= base) & (lane < base + 3)
    return jnp.where(is_term, terms, jnp.where(lane < 6, 1.0, 0.0))


def _fox_body(q_ref, k_ref, v_ref, c_ref, o_ref, qa_ref, ka_ref, va_ref, *scratch):
    nh = HEADS_PER_STEP
    m_refs, acc_refs = scratch[:nh], scratch[nh:]
    g = pl.program_id(1)
    i = pl.program_id(2)
    tq = q_ref.shape[0]

    @pl.when(i == 0)
    def _():
        _fill_value_scratch(va_ref, v_ref)
        cc = c_ref[...]
        for a in range(nh):
            ka_ref[a, :, :HEAD_DIM] = k_ref[:, _head_cols(a)]
            ka_ref[a, :, HEAD_DIM:] = _fox_bias_factors(cc, g * nh + a, True).astype(BF16)

    cc = c_ref[pl.ds(pl.multiple_of(i * tq, tq), tq), :]
    for a in range(nh):
        qa_ref[a, :, :HEAD_DIM] = q_ref[:, _head_cols(a)]
        qa_ref[a, :, HEAD_DIM:] = _fox_bias_factors(cc, g * nh + a, False).astype(BF16)

    _causal_flash(i, qa_ref, ka_ref, va_ref, m_refs, acc_refs, o_ref)


def fox_attention(qkv, c_col, *, batch, seq, n_heads, tq):
    nq = seq // tq
    nh = HEADS_PER_STEP
    groups = n_heads // nh
    width = nh * HEAD_DIM
    return pl.pallas_call(
        _fox_body,
        grid=(batch, groups, nq),
        in_specs=[pl.BlockSpec((tq, width), lambda b, g, i: (b * nq + i, g)),
                  pl.BlockSpec((seq, width), lambda b, g, i: (b, groups + g)),
                  pl.BlockSpec((seq, width), lambda b, g, i: (b, 2 * groups + g)),
                  pl.BlockSpec((seq, LANES), lambda b, g, i: (b, 0))],
        out_specs=pl.BlockSpec((tq, width), lambda b, g, i: (b * nq + i, g)),
        out_shape=jax.ShapeDtypeStruct((batch * seq, n_heads * HEAD_DIM), BF16),
        scratch_shapes=[pltpu.VMEM((nh, tq, 2 * HEAD_DIM), BF16),
                        pltpu.VMEM((nh, seq, 2 * HEAD_DIM), BF16),
                        pltpu.VMEM((nh, seq, 2 * HEAD_DIM), BF16)] + _softmax_scratch(tq, nh),
        compiler_params=_params(("parallel", "parallel", "arbitrary"), 58),
        name="fox_attention",
    )(qkv, qkv, qkv, c_col)


def _mem_attn_body(x_ref, gpre_ref, wq_ref, kv_ref, wo_ref, gpost_ref, o_ref):
    x = x_ref[...]
    h = _rms(x, gpre_ref[...]).astype(BF16)
    q = _dot(h, wq_ref[...]).astype(BF16)
    width = MEM_HEADS * MEM_HEAD_DIM
    outs = []
    for hd in range(MEM_HEADS):
        lo, hi = hd * MEM_HEAD_DIM, (hd + 1) * MEM_HEAD_DIM
        s = _dot_nt(q[:, lo:hi], kv_ref[:, lo:hi])
        p = jnp.exp(s - jnp.max(s, axis=1, keepdims=True))
        o = _dot(p.astype(BF16), kv_ref[:, width + lo:width + hi])
        outs.append((o / jnp.sum(p, axis=1, keepdims=True)).astype(BF16))
    y = _dot(jnp.concatenate(outs, axis=1), wo_ref[...])
    o_ref[...] = x + _rms(y, gpost_ref[...])


def mem_attention(x, gpre, wq, kv, wo, layer, gpost, *, seq, mem_tokens, tm):
    T, D = x.shape
    width = wq.shape[2]
    per_seq = seq // tm
    return pl.pallas_call(
        _mem_attn_body,
        grid=(T // tm,),
        in_specs=[pl.BlockSpec((tm, D), lambda i: (i, 0)),
                  pl.BlockSpec((1, D), lambda i: (0, 0)),
                  pl.BlockSpec((None, D, width), lambda i: (layer, 0, 0)),
                  pl.BlockSpec((mem_tokens, 2 * width), lambda i: (i // per_seq, 0)),
                  pl.BlockSpec((None, width, D), lambda i: (layer, 0, 0)),
                  pl.BlockSpec((1, D), lambda i: (0, 0))],
        out_specs=pl.BlockSpec((tm, D), lambda i: (i, 0)),
        out_shape=jax.ShapeDtypeStruct((T, D), F32),
        compiler_params=_params(("parallel",), 48),
        name="mem_attention",
    )(x, gpre.reshape(1, D), wq, kv, wo, gpost.reshape(1, D))


def _ffn_body(x_ref, gpre_ref, wg_ref, wv_ref, cwg_ref, cwv_ref, cbg_ref, cbv_ref, wd_ref, gpost_ref,
              o_ref, h_ref, acc_ref, ubuf_ref, carry_ref, *, tiles_per_seq):
    i = pl.program_id(0)
    f = pl.program_id(1)
    tm = x_ref.shape[0]
    halo = SUBLANES

    @pl.when(f == 0)
    def _():
        h_ref[...] = _rms(x_ref[...], gpre_ref[...]).astype(BF16)
        acc_ref[...] = jnp.zeros(acc_ref.shape, F32)

    seq_start = i % tiles_per_seq == 0
    tf = wg_ref.shape[1]

    u = _dot(h_ref[...], jnp.concatenate([wg_ref[...], wv_ref[...]], axis=1))
    ubuf_ref[0:halo, :] = jnp.where(seq_start, 0.0, carry_ref[f])
    ubuf_ref[halo:, :] = u
    carry_ref[f] = u[tm - halo:, :]
    cw = jnp.concatenate([cwg_ref[...], cwv_ref[...]], axis=1)
    cb = jnp.concatenate([cbg_ref[...], cbv_ref[...]], axis=1)
    uc = (cb + cw[0:1] * ubuf_ref[halo - 2:halo - 2 + tm, :]
          + cw[1:2] * ubuf_ref[halo - 1:halo - 1 + tm, :] + cw[2:3] * u)
    g, v = uc[:, :tf], uc[:, tf:]
    a = (g / (1.0 + jnp.exp(-g)) * v).astype(BF16)
    acc_ref[...] += _dot(a, wd_ref[...])

    @pl.when(f == pl.num_programs(1) - 1)
    def _():
        o_ref[...] = x_ref[...] + _rms(acc_ref[...], gpost_ref[...])


def conv_ffn(x, gpre, w_up, conv_w, conv_b, w_down, layer, gpost, *, seq, tm, tf):
    T, D = x.shape
    dff = w_down.shape[1]
    nf = dff // tf
    return pl.pallas_call(
        functools.partial(_ffn_body, tiles_per_seq=seq // tm),
        grid=(T // tm, nf),
        in_specs=[pl.BlockSpec((tm, D), lambda i, f: (i, 0)),
                  pl.BlockSpec((1, D), lambda i, f: (0, 0)),
                  pl.BlockSpec((None, D, tf), lambda i, f: (layer, 0, f)),
                  pl.BlockSpec((None, D, tf), lambda i, f: (layer, 0, f + nf)),
                  pl.BlockSpec((None, CONV_WIDTH, tf), lambda i, f: (layer, 0, f)),
                  pl.BlockSpec((None, CONV_WIDTH, tf), lambda i, f: (layer, 0, f + nf)),
                  pl.BlockSpec((None, 1, tf), lambda i, f: (layer, 0, f)),
                  pl.BlockSpec((None, 1, tf), lambda i, f: (layer, 0, f + nf)),
                  pl.BlockSpec((None, tf, D), lambda i, f: (layer, f, 0)),
                  pl.BlockSpec((1, D), lambda i, f: (0, 0))],
        out_specs=pl.BlockSpec((tm, D), lambda i, f: (i, 0)),
        out_shape=jax.ShapeDtypeStruct((T, D), F32),
        scratch_shapes=[pltpu.VMEM((tm, D), BF16),
                        pltpu.VMEM((tm, D), F32),
                        pltpu.VMEM((SUBLANES + tm, 2 * tf), F32),
                        pltpu.VMEM((nf, SUBLANES, 2 * tf), F32)],
        compiler_params=_params(("arbitrary", "arbitrary"), 56),
        name="conv_ffn",
    )(x, gpre.reshape(1, D), w_up, w_up, conv_w, conv_w, conv_b, conv_b, w_down, gpost.reshape(1, D))


def _rope_head_permutation():
    half = ROT_DIM // 2
    mid = HEAD_DIM // 2
    return np.concatenate([np.arange(0, half), np.arange(ROT_DIM, mid + half),
                           np.arange(half, ROT_DIM), np.arange(mid + half, HEAD_DIM)])


def _rope_tables(positions):
    half = ROT_DIM // 2
    mid = HEAD_DIM // 2
    inv_freq = ROPE_THETA ** (-jnp.arange(0, ROT_DIM, 2, dtype=F32) / ROT_DIM)
    ang = positions.astype(F32).reshape(-1, 1) * inv_freq
    cos, sin = jnp.cos(ang), jnp.sin(ang)
    T = ang.shape[0]
    ones = jnp.ones((T, mid - half), F32)
    zeros = jnp.zeros((T, mid - half), F32)
    c = jnp.concatenate([cos, ones, cos, ones], axis=1)
    s = jnp.concatenate([-sin, zeros, sin, zeros], axis=1)
    return c, s


def _scale_cast_body(w_ref, s_ref, o_ref):
    o_ref[...] = (w_ref[...] * s_ref[...]).astype(o_ref.dtype)


def scale_cast_columns(w, col_scale, *, tn):
    n_layers, D, _ = w.shape
    N = col_scale.shape[1]
    return pl.pallas_call(
        _scale_cast_body,
        grid=(n_layers, N // tn),
        in_specs=[pl.BlockSpec((None, D, tn), lambda l, j: (l, 0, j)),
                  pl.BlockSpec((1, tn), lambda l, j: (0, j))],
        out_specs=pl.BlockSpec((None, D, tn), lambda l, j: (l, 0, j)),
        out_shape=jax.ShapeDtypeStruct((n_layers, D, N), BF16),
        compiler_params=_params(("parallel", "parallel"), 32),
        name="scale_cast_columns",
    )(w, col_scale)


def _prepare_even_w_in(w_in_ab):
    n_layers = w_in_ab.shape[0]
    part = N_HEADS_A * HEAD_DIM
    scale = HEAD_DIM ** -0.5 * LOG2E
    col_scale = np.ones((6, part), np.float32)
    col_scale[[0, 3]] = scale
    wb = (w_in_ab * col_scale.reshape(1, 1, 6 * part)).astype(BF16)
    perm = _rope_head_permutation()
    pmat = np.zeros((HEAD_DIM, HEAD_DIM), np.float32)
    pmat[perm, np.arange(HEAD_DIM)] = 1.0

    def permute(cols):
        heads = cols.reshape(n_layers, D_MODEL, -1, HEAD_DIM)
        out = jnp.einsum('ldhe,ef->ldhf', heads, jnp.asarray(pmat, BF16), preferred_element_type=BF16)
        return out.reshape(cols.shape)

    return jnp.concatenate([permute(wb[..., :2 * part]), wb[..., 2 * part:3 * part],
                            permute(wb[..., 3 * part:5 * part]), wb[..., 5 * part:]], axis=-1)


def _even_mixer(x, g_pre, w_in, w_out, layer, g_post, rope, *, batch, seq):
    part = N_HEADS_A * HEAD_DIM
    qkv = norm_proj(x, g_pre, w_in, layer, tm=512, tn=part, out_dtype=BF16, rope=rope, rope_period=3)
    hb = N_HEADS_A
    oa = dilated_attention(qkv, batch=batch, seq=seq, n_heads=N_HEADS_A,
                           q_col=0, k_col=hb, v_col=2 * hb, tq=512)
    ob = moba_attention(qkv, batch=batch, seq=seq, n_heads=N_HEADS_B,
                        q_col=3 * hb, k_col=4 * hb, v_col=5 * hb, tq=512)
    return proj_norm_res([oa, ob], w_out, layer, x, g_post, tm=512)


def _odd_mixer(x, g_pre, w_qkv, w_f, b_f, w_out, layer, g_post, *, batch, seq):
    qkv = norm_proj(x, g_pre, w_qkv, layer, tm=512, tn=1024, out_dtype=BF16)
    bf = jnp.pad(b_f.astype(F32), (0, LANES - N_HEADS)).reshape(1, LANES)
    c_col = fox_gate_cumsum(x, g_pre, w_f, layer, bf, batch=batch, seq=seq, tm=512)
    o = fox_attention(qkv, c_col, batch=batch, seq=seq, n_heads=N_HEADS, tq=512)
    return proj_norm_res([o], w_out, layer, x, g_post, tm=512)


def kernel(x, mem, positions, w_in_ab, w_out_ab, w_in_c, b_f, w_out_c, g_mix_pre, g_mix_post, g_mem_kv, g_mem_pre, g_mem_post, w_mq, w_mk, w_mv, w_mo, g_ffn_pre, g_ffn_post, w_up, conv_w, conv_b, w_down):
    batch, seq, d = x.shape
    mem_tokens = mem.shape[1]
    depth = g_mix_pre.shape[0]
    rope = _rope_tables(positions)
    xs = x.reshape(batch * seq, d)
    mems = mem.reshape(batch * mem_tokens, d)

    nqkv = 3 * D_MODEL
    q_scale = np.ones((1, nqkv), np.float32)
    q_scale[:, :D_MODEL] = HEAD_DIM ** -0.5 * LOG2E
    w_in_even = _prepare_even_w_in(w_in_ab)
    w_qkv_odd = scale_cast_columns(w_in_c, jnp.asarray(q_scale), tn=1024)
    w_f_odd = jnp.pad(w_in_c[..., nqkv:], ((0, 0), (0, 0), (0, LANES - N_HEADS))).astype(BF16)
    w_out_even, w_out_odd = w_out_ab.astype(BF16), w_out_c.astype(BF16)
    w_mem_q = (w_mq * MEM_HEAD_DIM ** -0.5).astype(BF16)
    w_mem_kv = jnp.concatenate([w_mk, w_mv], axis=2).astype(BF16)
    w_mem_o = w_mo.astype(BF16)
    w_up_b, w_down_b = w_up.astype(BF16), w_down.astype(BF16)
    conv_b3 = conv_b.reshape(depth, 1, conv_b.shape[1])

    for layer in range(depth):
        if layer % 2 == 0:
            xs = _even_mixer(xs, g_mix_pre[layer], w_in_even, w_out_even, layer // 2,
                             g_mix_post[layer], rope, batch=batch, seq=seq)
        else:
            xs = _odd_mixer(xs, g_mix_pre[layer], w_qkv_odd, w_f_odd, b_f[layer // 2], w_out_odd, layer // 2,
                            g_mix_post[layer], batch=batch, seq=seq)
        kv = norm_proj(mems, g_mem_kv[layer], w_mem_kv, layer, tm=512, tn=w_mem_kv.shape[2], out_dtype=BF16)
        xs = mem_attention(xs, g_mem_pre[layer], w_mem_q, kv, w_mem_o, layer, g_mem_post[layer],
                           seq=seq, mem_tokens=mem_tokens, tm=512)
        xs = conv_ffn(xs, g_ffn_pre[layer], w_up_b, conv_w, conv_b3, w_down_b, layer, g_ffn_post[layer],
                      seq=seq, tm=512, tf=512)
    return xs.reshape(batch, seq, d)
```
